```python
import jax, jax.numpy as jnp
from jax import lax
import numpy as np

D_MODEL = 1024
BATCH = 16
SEQ = 2048
DEPTH = 2

CHUNK = 64
Q_BLOCK = 128
EPS = 1e-6
NEG = -1e30

A_HEADS = 6
A_NOPE = 64
A_ROPE = 32
A_V = 64
A_Q_RANK = 384
A_KV_RANK = 256
A_WIDTH = A_HEADS * A_V
ROPE_THETA = 10000.0

B_HEADS = 5
B_HD = 64
B_WIDTH = B_HEADS * B_HD
B_LEFT_CHUNKS = 8
B_BAND = (B_LEFT_CHUNKS + 1) * CHUNK
REL_CLIP = 128

C_HEADS = 5
C_HD = 64
C_WIDTH = C_HEADS * C_HD
FORGET_BIAS_INIT = 2.0

D_MIX = A_WIDTH + B_WIDTH + C_WIDTH

IN_SIZES = (A_Q_RANK, A_KV_RANK, A_ROPE, A_WIDTH,
            B_WIDTH, B_WIDTH, B_WIDTH, B_WIDTH,
            C_WIDTH, C_WIDTH, C_WIDTH, C_HEADS, C_WIDTH)
N_IN = (A_Q_RANK + A_KV_RANK + A_ROPE + A_WIDTH + 4 * B_WIDTH + 4 * C_WIDTH + C_HEADS)

kernel_name = "hybrid_mla_chunkband_fox_encoder"


def rmsnorm(x, g):
    xf = x.astype(jnp.float32)
    y = xf * lax.rsqrt(jnp.mean(xf * xf, axis=-1, keepdims=True) + EPS)
    return (y * g.astype(jnp.float32)).astype(x.dtype)


def split_cols(z, sizes):
    idx, o = [], 0
    for s in sizes[:-1]:
        o += s
        idx.append(o)
    return jnp.split(z, idx, axis=-1)


def rope_tables(positions):
    inv = ROPE_THETA ** (-jnp.arange(0, A_ROPE, 2, dtype=jnp.float32) / A_ROPE)
    ang = positions.astype(jnp.float32)[..., None] * inv
    return jnp.cos(ang), jnp.sin(ang)


def apply_rope(x, cos, sin):
    x1, x2 = jnp.split(x.astype(jnp.float32), 2, axis=-1)
    out = jnp.concatenate([x1 * cos - x2 * sin, x1 * sin + x2 * cos], axis=-1)
    return out.astype(x.dtype)


def attend(s, mask, v):
    p = jax.nn.softmax(jnp.where(mask, s, NEG), axis=-1)
    return jnp.einsum('bhqk,bhkd->bhqd', p.astype(v.dtype), v)


def mla_mixer(c_q, c_kv, k_pe, q_norm_g, w_uq, kv_norm_g, w_ukv, cos, sin):
    Bn, S, _ = c_q.shape
    q = (rmsnorm(c_q, q_norm_g) @ w_uq).reshape(Bn, S, A_HEADS, A_NOPE + A_ROPE)
    q_nope, q_pe = q[..., :A_NOPE], q[..., A_NOPE:]
    q_pe = apply_rope(q_pe, cos[:, :, None, :], sin[:, :, None, :])
    kv = (rmsnorm(c_kv, kv_norm_g) @ w_ukv).reshape(Bn, S, A_HEADS, A_NOPE + A_V)
    k_nope, v = kv[..., :A_NOPE], kv[..., A_NOPE:]
    k_pe = apply_rope(k_pe, cos, sin)
    k_pe = jnp.broadcast_to(k_pe[:, :, None, :], (Bn, S, A_HEADS, A_ROPE))
    q = jnp.concatenate([q_nope, q_pe], axis=-1).transpose(0, 2, 1, 3)
    k = jnp.concatenate([k_nope, k_pe], axis=-1).transpose(0, 2, 1, 3)
    v = v.transpose(0, 2, 1, 3)
    scale = (A_NOPE + A_ROPE) ** -0.5
    outs = []
    for i in range(S // Q_BLOCK):
        q0 = i * Q_BLOCK
        kend = q0 + Q_BLOCK
        s = jnp.einsum('bhqd,bhkd->bhqk', q[:, :, q0:kend], k[:, :, :kend],
                       preferred_element_type=jnp.float32) * scale
        q_chunk = (q0 + jnp.arange(Q_BLOCK)) // CHUNK
        k_chunk = jnp.arange(kend) // CHUNK
        mask = k_chunk[None, :] <= q_chunk[:, None]
        outs.append(attend(s, mask, v[:, :, :kend]))
    o = jnp.concatenate(outs, axis=2)
    return o.transpose(0, 2, 1, 3).reshape(Bn, S, A_WIDTH)


def chunk_band_mixer(q, k, v, rel_bias):
    Bn, S, _ = q.shape
    NC = S // CHUNK
    qc = q.reshape(Bn, NC, CHUNK, B_HEADS, B_HD)

    def band(t):
        t = t.reshape(Bn, S, B_HEADS, B_HD)
        t = jnp.pad(t, ((0, 0), (B_LEFT_CHUNKS * CHUNK, 0), (0, 0), (0, 0)))
        t = t.reshape(Bn, NC + B_LEFT_CHUNKS, CHUNK, B_HEADS, B_HD)
        t = jnp.stack([t[:, j:j + NC] for j in range(B_LEFT_CHUNKS + 1)], axis=2)
        return t.reshape(Bn, NC, B_BAND, B_HEADS, B_HD)

    kb, vb = band(k), band(v)
    s = jnp.einsum('bnqhd,bnkhd->bnhqk', qc, kb,
                   preferred_element_type=jnp.float32) * (B_HD ** -0.5)
    rel = (B_LEFT_CHUNKS * CHUNK + jnp.arange(CHUNK))[:, None] - jnp.arange(B_BAND)[None, :]
    idx = jnp.clip(rel, -REL_CLIP, REL_CLIP) + REL_CLIP
    s = s + rel_bias[:, idx].astype(jnp.float32)
    k_chunk = jnp.arange(NC)[:, None] - B_LEFT_CHUNKS + (jnp.arange(B_BAND) // CHUNK)[None, :]
    mask = (k_chunk >= 0)[None, :, None, None, :]
    p = jax.nn.softmax(jnp.where(mask, s, NEG), axis=-1)
    o = jnp.einsum('bnhqk,bnkhd->bnqhd', p.astype(vb.dtype), vb)
    return o.reshape(Bn, S, B_WIDTH)


def forgetting_mixer(q, k, v, f_logit, f_bias):
    Bn, S, _ = q.shape
    q = q.reshape(Bn, S, C_HEADS, C_HD).transpose(0, 2, 1, 3)
    k = k.reshape(Bn, S, C_HEADS, C_HD).transpose(0, 2, 1, 3)
    v = v.reshape(Bn, S, C_HEADS, C_HD).transpose(0, 2, 1, 3)
    log_f = jax.nn.log_sigmoid(f_logit.astype(jnp.float32) + f_bias.astype(jnp.float32))
    F = jnp.cumsum(log_f, axis=1).transpose(0, 2, 1)
    scale = C_HD ** -0.5
    outs = []
    for i in range(S // Q_BLOCK):
        q0 = i * Q_BLOCK
        kend = q0 + Q_BLOCK
        s = jnp.einsum('bhqd,bhkd->bhqk', q[:, :, q0:kend], k[:, :, :kend],
                       preferred_element_type=jnp.float32) * scale
        s = s + F[:, :, q0:kend, None] - F[:, :, None, :kend]
        mask = jnp.arange(kend)[None, :] <= (q0 + jnp.arange(Q_BLOCK))[:, None]
        outs.append(attend(s, mask, v[:, :, :kend]))
    o = jnp.concatenate(outs, axis=2)
    return o.transpose(0, 2, 1, 3).reshape(Bn, S, C_WIDTH)


def hybrid_layer(x, c_act, cos, sin, w_ada, b_ada, norm_g, w_in, a_q_norm_g, a_w_uq,
                 a_kv_norm_g, a_w_ukv, b_rel_bias, c_forget_b, w_out):
    mod = c_act @ w_ada + b_ada
    shift, scale, gate = jnp.split(mod, 3, axis=-1)
    h = rmsnorm(x, norm_g) * (1.0 + scale[:, None, :]) + shift[:, None, :]
    z = h @ w_in
    (a_cq, a_ckv, a_kpe, a_gate,
     b_q, b_k, b_v, b_gate,
     c_q, c_k, c_v, c_f, c_gate) = split_cols(z, IN_SIZES)
    a_out = mla_mixer(a_cq, a_ckv, a_kpe, a_q_norm_g, a_w_uq, a_kv_norm_g, a_w_ukv, cos, sin)
    b_out = chunk_band_mixer(b_q, b_k, b_v, b_rel_bias)
    c_out = forgetting_mixer(c_q, c_k, c_v, c_f, c_forget_b)
    y = jnp.concatenate([a_out * jax.nn.silu(a_gate),
                         b_out * jax.nn.silu(b_gate),
                         c_out * jax.nn.silu(c_gate)], axis=-1) @ w_out
    return x + gate[:, None, :] * y


def setup_inputs(seed: int = 0) -> dict:
    key = jax.random.key(seed)
    ks = jax.random.split(key, 16)
    f32 = jnp.float32
    x = jax.random.normal(ks[0], (BATCH, SEQ, D_MODEL), f32)
    c = jax.random.normal(ks[1], (BATCH, D_MODEL), f32)
    start = jax.random.randint(ks[2], (BATCH, 1), 0, 64, dtype=jnp.int32) * CHUNK
    positions = (start + jnp.arange(SEQ, dtype=jnp.int32)[None, :]).astype(jnp.int32)
    w_ada = jax.random.normal(ks[3], (DEPTH, D_MODEL, 3 * D_MODEL), f32) * D_MODEL ** -0.5
    b_ada = jax.random.normal(ks[4], (DEPTH, 3 * D_MODEL), f32) * 0.02
    norm_g = 1.0 + 0.02 * jax.random.normal(ks[5], (DEPTH, D_MODEL), f32)
    w_in = jax.random.normal(ks[6], (DEPTH, D_MODEL, N_IN), f32) * D_MODEL ** -0.5
    a_q_norm_g = 1.0 + 0.02 * jax.random.normal(ks[7], (DEPTH, A_Q_RANK), f32)
    a_w_uq = jax.random.normal(ks[8], (DEPTH, A_Q_RANK, A_HEADS * (A_NOPE + A_ROPE)), f32) * A_Q_RANK ** -0.5
    a_kv_norm_g = 1.0 + 0.02 * jax.random.normal(ks[9], (DEPTH, A_KV_RANK), f32)
    a_w_ukv = jax.random.normal(ks[10], (DEPTH, A_KV_RANK, A_HEADS * (A_NOPE + A_V)), f32) * A_KV_RANK ** -0.5
    b_rel_bias = 0.2 * jax.random.normal(ks[11], (DEPTH, B_HEADS, 2 * REL_CLIP + 1), f32)
    c_forget_b = FORGET_BIAS_INIT + 0.1 * jax.random.normal(ks[12], (DEPTH, C_HEADS), f32)
    w_out = jax.random.normal(ks[13], (DEPTH, D_MIX, D_MODEL), f32) * D_MIX ** -0.5
    final_g = 1.0 + 0.02 * jax.random.normal(ks[14], (D_MODEL,), f32)
    return {"x": x, "c": c, "positions": positions, "w_ada": w_ada, "b_ada": b_ada,
            "norm_g": norm_g, "w_in": w_in, "a_q_norm_g": a_q_norm_g, "a_w_uq": a_w_uq,
            "a_kv_norm_g": a_kv_norm_g, "a_w_ukv": a_w_ukv, "b_rel_bias": b_rel_bias,
            "c_forget_b": c_forget_b, "w_out": w_out, "final_g": final_g}


def reference(x, c, positions, w_ada, b_ada, norm_g, w_in, a_q_norm_g, a_w_uq,
              a_kv_norm_g, a_w_ukv, b_rel_bias, c_forget_b, w_out, final_g):
    cos, sin = rope_tables(positions)
    c_act = jax.nn.silu(c)
    for l in range(DEPTH):
        x = hybrid_layer(x, c_act, cos, sin, w_ada[l], b_ada[l], norm_g[l], w_in[l],
                         a_q_norm_g[l], a_w_uq[l], a_kv_norm_g[l], a_w_ukv[l],
                         b_rel_bias[l], c_forget_b[l], w_out[l])
    return rmsnorm(x, final_g)
```

```python
import functools

import numpy as np
import jax
import jax.numpy as jnp
from jax import lax
from jax.experimental import pallas as pl
from jax.experimental.pallas import tpu as pltpu

F32 = jnp.float32
BF16 = jnp.bfloat16

D_MODEL = 1024
BATCH = 16
SEQ = 2048
DEPTH = 2
CHUNK = 64
EPS = 1e-6
NEG = -1e30

A_HEADS, A_NOPE, A_ROPE, A_V = 6, 64, 32, 64
A_Q_RANK, A_KV_RANK = 384, 256
A_WIDTH = A_HEADS * A_V
ROPE_THETA = 10000.0
B_HEADS, B_HD = 5, 64
B_WIDTH = B_HEADS * B_HD
B_LEFT_CHUNKS = 8
REL_CLIP = 128
C_HEADS, C_HD = 5, 64
C_WIDTH = C_HEADS * C_HD
D_MIX = A_WIDTH + B_WIDTH + C_WIDTH

_SIZES = (A_Q_RANK, A_KV_RANK, A_ROPE, A_WIDTH, B_WIDTH, B_WIDTH, B_WIDTH, B_WIDTH,
          C_WIDTH, C_WIDTH, C_WIDTH, C_HEADS, C_WIDTH)
_OFFS = tuple(int(v) for v in np.cumsum((0,) + _SIZES))

HP = 128
TQ = 256
NT = SEQ // TQ
TM_IN = 256
TM_OUT = 512
HALF = A_ROPE // 2
AUG = HP // 2
B_BAND_TILES = (B_LEFT_CHUNKS * CHUNK) // TQ

V7X_VMEM_LIMIT_BYTES = 56 * 1024 * 1024


def _mm(a, b):
    return jnp.dot(a, b, preferred_element_type=F32)


def _nt(a, b):
    return lax.dot_general(a, b, (((1,), (1,)), ((), ())), preferred_element_type=F32)


def _rms(x, g):
    return x * lax.rsqrt(jnp.mean(x * x, axis=-1, keepdims=True) + EPS) * g


def _silu(x):
    return x / (1.0 + jnp.exp(-x))


def _split3(x):
    p1 = x.astype(BF16)
    r1 = x - p1.astype(F32)
    p2 = r1.astype(BF16)
    p3 = (r1 - p2.astype(F32)).astype(BF16)
    return p1, p2, p3


def _mod_kernel(c_ref, w_ref, b_ref, o_ref):
    ca = _silu(c_ref[...])
    w = w_ref[...]
    ch = ca.astype(BF16)
    cl = (ca - ch.astype(F32)).astype(BF16)
    wh = w.astype(BF16)
    wl = (w - wh.astype(F32)).astype(BF16)
    o_ref[...] = _mm(ch, wh) + _mm(ch, wl) + _mm(cl, wh) + b_ref[...]


def _modulation(c, w_ada, b_ada):
    return pl.pallas_call(
        _mod_kernel,
        grid=(DEPTH, 3),
        in_specs=[
            pl.BlockSpec((BATCH, D_MODEL), lambda l, j: (0, 0)),
            pl.BlockSpec((None, D_MODEL, D_MODEL), lambda l, j: (l, 0, j)),
            pl.BlockSpec((None, None, 1, D_MODEL), lambda l, j: (l, j, 0, 0)),
        ],
        out_specs=pl.BlockSpec((None, None, BATCH, D_MODEL), lambda l, j: (l, j, 0, 0)),
        out_shape=jax.ShapeDtypeStruct((DEPTH, 3, BATCH, D_MODEL), F32),
        name="adaln_mod",
    )(c, w_ada, b_ada.reshape(DEPTH, 3, 1, D_MODEL))


def _store_fm(ref, val):
    for t in range(TM_IN // TQ):
        ref[t] = val[:, t * TQ:(t + 1) * TQ].astype(ref.dtype)


def _in_kernel(x_ref, shift_ref, scale_ref, g_ref, pos_ref, inv_ref,
               wcq_ref, wckv_ref, wkpeT_ref, wgT_ref,
               wqBT_ref, wkB_ref, wvBT_ref, wqCT_ref, wkC_ref, wvCT_ref, wcf_ref,
               qng_ref, wuqT_ref, kvng_ref, wuk_ref, wuvT_ref,
               fb_ref, tri_ref, pk_ref, pq_ref, onesk_ref, onesq_ref,
               qA_ref, kA_ref, vA_ref, qB_ref, kB_ref, vB_ref, qC_ref, kC_ref, vC_ref,
               gate_ref, carry_ref):
    x = x_ref[...]
    h = _rms(x, g_ref[...]) * (1.0 + scale_ref[...]) + shift_ref[...]
    hb = h.astype(BF16)

    for r in range(0, D_MIX, 256):
        gt = _nt(wgT_ref[r:r + 256, :], hb)
        for t in range(TM_IN // TQ):
            gate_ref[t, r:r + 256, :] = _silu(gt[:, t * TQ:(t + 1) * TQ]).astype(BF16)

    ang = inv_ref[...] * pos_ref[...].astype(F32)
    cos = jnp.cos(ang)
    sin = jnp.sin(ang)

    cqn = _rms(_mm(hb, wcq_ref[...]), qng_ref[...]).astype(BF16)
    a_scale = (A_NOPE + A_ROPE) ** -0.5
    for hh in range(A_HEADS):
        base = hh * HP
        qh = _nt(wuqT_ref[base:base + HP, :], cqn) * a_scale
        x1 = qh[A_NOPE:A_NOPE + HALF]
        x2 = qh[A_NOPE + HALF:A_NOPE + A_ROPE]
        blk = jnp.concatenate(
            [qh[:A_NOPE], x1 * cos - x2 * sin, x1 * sin + x2 * cos,
             jnp.zeros((HP - A_NOPE - A_ROPE, TM_IN), F32)], axis=0)
        for t in range(TM_IN // TQ):
            qA_ref[t, base:base + HP, :] = blk[:, t * TQ:(t + 1) * TQ].astype(BF16)

    ckvn = _rms(_mm(hb, wckv_ref[...]), kvng_ref[...]).astype(BF16)
    _store_fm(vA_ref, _nt(wuvT_ref[...], ckvn))
    kpeT = _nt(wkpeT_ref[...], hb)
    x1 = kpeT[A_NOPE:A_NOPE + HALF]
    x2 = kpeT[A_NOPE + HALF:A_NOPE + A_ROPE]
    kpe_tok = jnp.concatenate(
        [jnp.zeros((A_NOPE, TM_IN), F32), x1 * cos - x2 * sin, x1 * sin + x2 * cos,
         jnp.zeros((HP - A_NOPE - A_ROPE, TM_IN), F32)], axis=0).T
    kn = _mm(ckvn, wuk_ref[...])
    for hh in range(A_HEADS):
        sl = slice(hh * HP, (hh + 1) * HP)
        kA_ref[:, sl] = (kn[:, sl] + kpe_tok).astype(BF16)

    _store_fm(qB_ref, _nt(wqBT_ref[...], hb))
    kB_ref[...] = _mm(hb, wkB_ref[...]).astype(BF16)
    _store_fm(vB_ref, _nt(wvBT_ref[...], hb))

    @pl.when(pl.program_id(1) == 0)
    def _():
        carry_ref[...] = jnp.zeros_like(carry_ref)

    cf = _mm(hb, wcf_ref[...]) + fb_ref[...]
    logf = -(jnp.maximum(-cf, 0.0) + jnp.log1p(jnp.exp(-jnp.abs(cf))))
    fl = _mm(tri_ref[...], jnp.concatenate(_split3(logf), axis=1))
    f_cum = fl[:, :HP] + fl[:, HP:2 * HP] + fl[:, 2 * HP:] + carry_ref[...]
    carry_ref[...] = f_cum[TM_IN - 1:TM_IN, :]
    fs = jnp.concatenate(_split3(f_cum), axis=1)
    kC_ref[...] = (_mm(hb, wkC_ref[...]) + _mm(fs, pk_ref[...]) + onesk_ref[...]).astype(BF16)
    _store_fm(qC_ref, _nt(wqCT_ref[...], hb) + _nt(pq_ref[...], fs) + onesq_ref[...])
    _store_fm(vC_ref, _nt(wvCT_ref[...], hb))


def _const_spec(shape):
    nd = len(shape)
    return pl.BlockSpec(shape, lambda b, i, _nd=nd: (0,) * _nd)


def _in_call(x, mod_l, pos3, consts, lw):
    nsub = TM_IN // TQ
    tok = lambda w: pl.BlockSpec((None, TM_IN, w), lambda b, i: (b, i, 0))
    fm = lambda r: pl.BlockSpec((None, nsub, r, TQ), lambda b, i: (b, i, 0, 0))
    weights = [lw[k] for k in ("wcq", "wckv", "wkpeT", "wgT", "wqBT", "wkB", "wvBT",
                               "wqCT", "wkC", "wvCT", "wcf", "qng", "wuqT", "kvng",
                               "wuk", "wuvT", "fb")]
    cst = [consts[k] for k in ("tri", "pk", "pq", "onesk", "onesq")]
    in_specs = [
        tok(D_MODEL),
        pl.BlockSpec((None, None, 1, D_MODEL), lambda b, i: (0, b, 0, 0)),
        pl.BlockSpec((None, None, 1, D_MODEL), lambda b, i: (1, b, 0, 0)),
        _const_spec((1, D_MODEL)),
        pl.BlockSpec((None, 1, TM_IN), lambda b, i: (b, 0, i)),
        _const_spec((HALF, 1)),
    ] + [_const_spec(w.shape) for w in weights] + [_const_spec(c.shape) for c in cst]
    fm_shape = lambda r: jax.ShapeDtypeStruct((BATCH, NT, r, TQ), BF16)
    tok_shape = lambda w: jax.ShapeDtypeStruct((BATCH, SEQ, w), BF16)
    out_shape = [fm_shape(A_HEADS * HP), tok_shape(A_HEADS * HP), fm_shape(A_WIDTH),
                 fm_shape(B_HEADS * HP), tok_shape(B_HEADS * HP), fm_shape(B_WIDTH),
                 fm_shape(C_HEADS * HP), tok_shape(C_HEADS * HP), fm_shape(C_WIDTH),
                 fm_shape(D_MIX)]
    out_specs = [fm(A_HEADS * HP), tok(A_HEADS * HP), fm(A_WIDTH),
                 fm(B_HEADS * HP), tok(B_HEADS * HP), fm(B_WIDTH),
                 fm(C_HEADS * HP), tok(C_HEADS * HP), fm(C_WIDTH),
                 fm(D_MIX)]
    return pl.pallas_call(
        _in_kernel,
        grid=(BATCH, SEQ // TM_IN),
        in_specs=in_specs,
        out_specs=out_specs,
        out_shape=out_shape,
        scratch_shapes=[pltpu.VMEM((1, HP), F32)],
        compiler_params=pltpu.CompilerParams(
            dimension_semantics=("parallel", "arbitrary"),
            vmem_limit_bytes=V7X_VMEM_LIMIT_BYTES),
        name="in_proj",
    )(x, mod_l, mod_l, lw["norm_g"], pos3, consts["inv"], *weights, *cst)


def _attn_kernel(qT_ref, k_ref, vT_ref, gate_ref, bias_ref, o_ref, *, band):
    def step(qT, j, start, carry, bias):
        m, l, acc = carry
        s = _mm(k_ref[pl.ds(start, TQ), :], qT)
        if bias is not None:
            s = s + bias
        m_new = jnp.maximum(m, jnp.max(s, axis=0, keepdims=True))
        alpha = jnp.exp(m - m_new)
        p = jnp.exp(s - m_new)
        l = alpha * l + jnp.sum(p, axis=0, keepdims=True)
        acc = alpha * acc + _mm(vT_ref[j], p.astype(BF16))
        return m_new, l, acc

    for i in range(NT):
        qT = qT_ref[i]
        carry = (jnp.full((1, TQ), NEG, F32), jnp.zeros((1, TQ), F32),
                 jnp.zeros((vT_ref.shape[1], TQ), F32))
        if band is None:
            carry = lax.fori_loop(
                0, i,
                lambda j, c: step(qT, j, pl.multiple_of(j * TQ, TQ), c, None), carry)
            carry = step(qT, i, i * TQ, carry, bias_ref[0])
        else:
            for d in range(min(i, band) + 1):
                carry = step(qT, i - d, (i - d) * TQ, carry, bias_ref[d])
        _, l, acc = carry
        o_ref[i] = (acc / l * gate_ref[i].astype(F32)).astype(BF16)


def _attn_call(qT, k, vT, gateT, bias, *, heads, hd, gate_row0, band, name):
    if band is None:
        bias_spec = pl.BlockSpec(bias.shape, lambda b, h: (0, 0, 0))
    else:
        bias_spec = pl.BlockSpec((None,) + bias.shape[1:], lambda b, h: (h, 0, 0, 0))
    g0 = gate_row0 // hd
    return pl.pallas_call(
        functools.partial(_attn_kernel, band=band),
        grid=(BATCH, heads),
        in_specs=[
            pl.BlockSpec((None, NT, HP, TQ), lambda b, h: (b, 0, h, 0)),
            pl.BlockSpec((None, SEQ, HP), lambda b, h: (b, 0, h)),
            pl.BlockSpec((None, NT, hd, TQ), lambda b, h: (b, 0, h, 0)),
            pl.BlockSpec((None, NT, hd, TQ), lambda b, h: (b, 0, g0 + h, 0)),
            bias_spec,
        ],
        out_specs=pl.BlockSpec((None, NT, hd, TQ), lambda b, h: (b, 0, h, 0)),
        out_shape=jax.ShapeDtypeStruct((BATCH, NT, heads * hd, TQ), BF16),
        compiler_params=pltpu.CompilerParams(dimension_semantics=("parallel", "parallel")),
        name=name,
    )(qT, k, vT, gateT, bias)


def _out_kernel(mA_ref, mB_ref, mC_ref, woT_ref, x_ref, gate_ref, fg_ref, o_ref, *, final):
    for t in range(TM_OUT // TQ):
        mixT = jnp.concatenate([mA_ref[t], mB_ref[t], mC_ref[t]], axis=0)
        y = _mm(woT_ref[...], mixT).T
        rows = slice(t * TQ, (t + 1) * TQ)
        xn = x_ref[rows, :] + gate_ref[...] * y
        if final:
            xn = _rms(xn, fg_ref[...])
        o_ref[rows, :] = xn


def _out_call(mA, mB, mC, woT, x, mod_l, final_g, *, final):
    nsub = TM_OUT // TQ
    fm = lambda r: pl.BlockSpec((None, nsub, r, TQ), lambda b, i: (b, i, 0, 0))
    tok = pl.BlockSpec((None, TM_OUT, D_MODEL), lambda b, i: (b, i, 0))
    return pl.pallas_call(
        functools.partial(_out_kernel, final=final),
        grid=(BATCH, SEQ // TM_OUT),
        in_specs=[fm(A_WIDTH), fm(B_WIDTH), fm(C_WIDTH),
                  _const_spec((D_MODEL, D_MIX)), tok,
                  pl.BlockSpec((None, None, 1, D_MODEL), lambda b, i: (2, b, 0, 0)),
                  _const_spec((1, D_MODEL))],
        out_specs=tok,
        out_shape=jax.ShapeDtypeStruct((BATCH, SEQ, D_MODEL), F32),
        compiler_params=pltpu.CompilerParams(dimension_semantics=("parallel", "parallel")),
        name="out_proj",
    )(mA, mB, mC, woT, x, mod_l, final_g)


def _pad_heads(w, heads, hd):
    k = w.shape[0]
    return jnp.pad(w.reshape(k, heads, hd), ((0, 0), (0, 0), (0, HP - hd))).reshape(k, heads * HP)


def _constants():
    tri = np.tril(np.ones((TM_IN, TM_IN), np.float32))
    pk = np.zeros((3 * HP, C_HEADS * HP), np.float32)
    pq = np.zeros((C_HEADS * HP, 3 * HP), np.float32)
    onesk = np.zeros((1, C_HEADS * HP), np.float32)
    onesq = np.zeros((C_HEADS * HP, 1), np.float32)
    for hh in range(C_HEADS):
        for j in range(3):
            onesk[0, hh * HP + AUG + j] = 1.0
            pq[hh * HP + AUG + j, j * HP + hh] = 1.0
            pk[j * HP + hh, hh * HP + AUG + 3 + j] = -1.0
            onesq[hh * HP + AUG + 3 + j, 0] = 1.0
    kk = np.arange(TQ)[:, None]
    qq = np.arange(TQ)[None, :]
    mask_a = np.where(kk // CHUNK <= qq // CHUNK, 0.0, NEG).astype(np.float32)[None]
    mask_c = np.where(kk <= qq, 0.0, NEG).astype(np.float32)[None]
    d = np.arange(B_BAND_TILES + 1)[:, None, None]
    rel = TQ * d + qq[None] - kk[None]
    cdist = (TQ // CHUNK) * d + qq[None] // CHUNK - kk[None] // CHUNK
    b_idx = np.clip(rel, -REL_CLIP, REL_CLIP) + REL_CLIP
    b_valid = (cdist >= 0) & (cdist <= B_LEFT_CHUNKS)
    inv = ROPE_THETA ** (-jnp.arange(0, A_ROPE, 2, dtype=F32) / A_ROPE)
    return {
        "tri": jnp.asarray(tri, BF16), "pk": jnp.asarray(pk, BF16), "pq": jnp.asarray(pq, BF16),
        "onesk": jnp.asarray(onesk), "onesq": jnp.asarray(onesq),
        "mask_a": jnp.asarray(mask_a), "mask_c": jnp.asarray(mask_c),
        "b_idx": b_idx, "b_valid": b_valid, "inv": inv.reshape(HALF, 1),
    }


def _layer_weights(l, norm_g, w_in, a_q_norm_g, a_w_uq, a_kv_norm_g, a_w_ukv,
                   b_rel_bias, c_forget_b, w_out, consts):
    w = w_in[l]
    col = lambda n: w[:, _OFFS[n]:_OFFS[n + 1]]
    w_gate = jnp.concatenate([col(3), col(7), col(12)], axis=1)
    uq = jnp.pad(a_w_uq[l].reshape(A_Q_RANK, A_HEADS, A_NOPE + A_ROPE),
                 ((0, 0), (0, 0), (0, HP - A_NOPE - A_ROPE))).reshape(A_Q_RANK, A_HEADS * HP)
    ukv = a_w_ukv[l].reshape(A_KV_RANK, A_HEADS, A_NOPE + A_V)
    uk = jnp.pad(ukv[:, :, :A_NOPE], ((0, 0), (0, 0), (0, HP - A_NOPE))).reshape(A_KV_RANK, A_HEADS * HP)
    uv = ukv[:, :, A_NOPE:].reshape(A_KV_RANK, A_WIDTH)
    kpeT = jnp.pad(col(2).T, ((A_NOPE, HP - A_NOPE - A_ROPE), (0, 0)))
    bias_b = jnp.where(consts["b_valid"][None], b_rel_bias[l][:, consts["b_idx"]], NEG)
    bf = lambda a: a.astype(BF16)
    return {
        "norm_g": norm_g[l].reshape(1, D_MODEL),
        "wcq": bf(col(0)), "wckv": bf(col(1)), "wkpeT": bf(kpeT), "wgT": bf(w_gate.T),
        "wqBT": bf((_pad_heads(col(4), B_HEADS, B_HD) * B_HD ** -0.5).T),
        "wkB": bf(_pad_heads(col(5), B_HEADS, B_HD)), "wvBT": bf(col(6).T),
        "wqCT": bf((_pad_heads(col(8), C_HEADS, C_HD) * C_HD ** -0.5).T),
        "wkC": bf(_pad_heads(col(9), C_HEADS, C_HD)), "wvCT": bf(col(10).T),
        "wcf": bf(jnp.pad(col(11), ((0, 0), (0, HP - C_HEADS)))),
        "qng": a_q_norm_g[l].reshape(1, A_Q_RANK), "wuqT": bf(uq.T),
        "kvng": a_kv_norm_g[l].reshape(1, A_KV_RANK), "wuk": bf(uk), "wuvT": bf(uv.T),
        "fb": jnp.pad(c_forget_b[l], (0, HP - C_HEADS)).reshape(1, HP),
        "woT": bf(w_out[l].T), "bias_b": bias_b.astype(F32),
    }


def kernel(x, c, positions, w_ada, b_ada, norm_g, w_in, a_q_norm_g, a_w_uq, a_kv_norm_g,
           a_w_ukv, b_rel_bias, c_forget_b, w_out, final_g):
    consts = _constants()
    mod = _modulation(c, w_ada, b_ada).reshape(DEPTH, 3, BATCH, 1, D_MODEL)
    pos3 = positions.reshape(BATCH, 1, SEQ)
    fg = final_g.reshape(1, D_MODEL)
    for l in range(DEPTH):
        lw = _layer_weights(l, norm_g, w_in, a_q_norm_g, a_w_uq, a_kv_norm_g, a_w_ukv,
                            b_rel_bias, c_forget_b, w_out, consts)
        qA, kA, vA, qB, kB, vB, qC, kC, vC, gateT = _in_call(x, mod[l], pos3, consts, lw)
        mA = _attn_call(qA, kA, vA, gateT, consts["mask_a"], heads=A_HEADS, hd=A_V,
                        gate_row0=0, band=None, name="attn_mla")
        mB = _attn_call(qB, kB, vB, gateT, lw["bias_b"], heads=B_HEADS, hd=B_HD,
                        gate_row0=A_WIDTH, band=B_BAND_TILES, name="attn_band")
        mC = _attn_call(qC, kC, vC, gateT, consts["mask_c"], heads=C_HEADS, hd=C_HD,
                        gate_row0=A_WIDTH + B_WIDTH, band=None, name="attn_forget")
        x = _out_call(mA, mB, mC, lw["woT"], x, mod[l], fg, final=(l == DEPTH - 1))
    return x
```

```python
import functools

import numpy as np
import jax
import jax.numpy as jnp
from jax import lax
from jax.experimental import pallas as pl
from jax.experimental.pallas import tpu as pltpu

F32 = jnp.float32
BF16 = jnp.bfloat16

D_MODEL = 1024
BATCH = 16
SEQ = 2048
DEPTH = 2
CHUNK = 64
EPS = 1e-6
NEG = -1e30

A_HEADS, A_NOPE, A_ROPE, A_V = 6, 64, 32, 64
A_Q_RANK, A_KV_RANK = 384, 256
A_WIDTH = A_HEADS * A_V
ROPE_THETA = 10000.0
B_HEADS, B_HD = 5, 64
B_WIDTH = B_HEADS * B_HD
B_LEFT_CHUNKS = 8
REL_CLIP = 128
C_HEADS, C_HD = 5, 64
C_WIDTH = C_HEADS * C_HD
D_MIX = A_WIDTH + B_WIDTH + C_WIDTH

_SIZES = (A_Q_RANK, A_KV_RANK, A_ROPE, A_WIDTH, B_WIDTH, B_WIDTH, B_WIDTH, B_WIDTH,
          C_WIDTH, C_WIDTH, C_WIDTH, C_HEADS, C_WIDTH)
_OFFS = tuple(int(v) for v in np.cumsum((0,) + _SIZES))

LOG2E = 1.4426950408889634
HP = 128
TQ = 256
NT = SEQ // TQ
TM_IN = 256
TM_OUT = 512
HALF = A_ROPE // 2
AUG = HP // 2
B_BAND_TILES = (B_LEFT_CHUNKS * CHUNK) // TQ

V7X_VMEM_LIMIT_BYTES = 56 * 1024 * 1024


def _mm(a, b):
    return jnp.dot(a, b, preferred_element_type=F32)


def _nt(a, b):
    return lax.dot_general(a, b, (((1,), (1,)), ((), ())), preferred_element_type=F32)


def _rms(x, g):
    return x * lax.rsqrt(jnp.mean(x * x, axis=-1, keepdims=True) + EPS) * g


def _silu(x):
    return x / (1.0 + jnp.exp(-x))


def _split3(x):
    p1 = x.astype(BF16)
    r1 = x - p1.astype(F32)
    p2 = r1.astype(BF16)
    p3 = (r1 - p2.astype(F32)).astype(BF16)
    return p1, p2, p3


def _mod_kernel(c_ref, w_ref, b_ref, o_ref):
    ca = _silu(c_ref[...])
    w = w_ref[...]
    ch = ca.astype(BF16)
    cl = (ca - ch.astype(F32)).astype(BF16)
    wh = w.astype(BF16)
    wl = (w - wh.astype(F32)).astype(BF16)
    o_ref[...] = _mm(ch, wh) + _mm(ch, wl) + _mm(cl, wh) + b_ref[...]


def _modulation(c, w_ada, b_ada):
    return pl.pallas_call(
        _mod_kernel,
        grid=(DEPTH, 3),
        in_specs=[
            pl.BlockSpec((BATCH, D_MODEL), lambda l, j: (0, 0)),
            pl.BlockSpec((None, D_MODEL, D_MODEL), lambda l, j: (l, 0, j)),
            pl.BlockSpec((None, None, 1, D_MODEL), lambda l, j: (l, j, 0, 0)),
        ],
        out_specs=pl.BlockSpec((None, None, BATCH, D_MODEL), lambda l, j: (l, j, 0, 0)),
        out_shape=jax.ShapeDtypeStruct((DEPTH, 3, BATCH, D_MODEL), F32),
        name="adaln_mod",
    )(c, w_ada, b_ada.reshape(DEPTH, 3, 1, D_MODEL))


def _in_kernel(x_ref, shift_ref, scale_ref, g_ref, pos_ref, inv_ref,
               wcq_ref, wckv_ref, wkpeT_ref, wgT_ref,
               wqBT_ref, wkB_ref, wvBT_ref, wqCT_ref, wkC_ref, wvCT_ref, wcf_ref,
               qng_ref, wuqT_ref, kvng_ref, wuk_ref, wuvT_ref,
               fb_ref, tri_ref, pk_ref, pq_ref, onesk_ref, onesq_ref,
               qA_ref, kA_ref, vA_ref, qB_ref, kB_ref, vB_ref, qC_ref, kC_ref, vC_ref,
               gate_ref, carry_ref):
    x = x_ref[...]
    h = _rms(x, g_ref[...]) * (1.0 + scale_ref[...]) + shift_ref[...]
    hb = h.astype(BF16)

    for r in range(0, D_MIX, 256):
        gate_ref[r:r + 256, :] = _silu(_nt(wgT_ref[r:r + 256, :], hb)).astype(BF16)

    ang = inv_ref[...] * pos_ref[...].astype(F32)
    cos = jnp.cos(ang)
    sin = jnp.sin(ang)

    cqn = _rms(_mm(hb, wcq_ref[...]), qng_ref[...]).astype(BF16)
    a_scale = (A_NOPE + A_ROPE) ** -0.5 * LOG2E
    for hh in range(A_HEADS):
        base = hh * HP
        qh = _nt(wuqT_ref[base:base + HP, :], cqn) * a_scale
        x1 = qh[A_NOPE:A_NOPE + HALF]
        x2 = qh[A_NOPE + HALF:A_NOPE + A_ROPE]
        qA_ref[base:base + HP, :] = jnp.concatenate(
            [qh[:A_NOPE], x1 * cos - x2 * sin, x1 * sin + x2 * cos,
             jnp.zeros((HP - A_NOPE - A_ROPE, TM_IN), F32)], axis=0).astype(BF16)

    ckvn = _rms(_mm(hb, wckv_ref[...]), kvng_ref[...]).astype(BF16)
    vA_ref[...] = _nt(wuvT_ref[...], ckvn).astype(BF16)
    kpeT = _nt(wkpeT_ref[...], hb)
    x1 = kpeT[A_NOPE:A_NOPE + HALF]
    x2 = kpeT[A_NOPE + HALF:A_NOPE + A_ROPE]
    kpe_tok = jnp.concatenate(
        [jnp.zeros((A_NOPE, TM_IN), F32), x1 * cos - x2 * sin, x1 * sin + x2 * cos,
         jnp.zeros((HP - A_NOPE - A_ROPE, TM_IN), F32)], axis=0).T
    kn = _mm(ckvn, wuk_ref[...])
    for hh in range(A_HEADS):
        sl = slice(hh * HP, (hh + 1) * HP)
        kA_ref[:, sl] = (kn[:, sl] + kpe_tok).astype(BF16)

    bc_scale = B_HD ** -0.5 * LOG2E
    qB_ref[...] = (_nt(wqBT_ref[...], hb) * bc_scale).astype(BF16)
    kB_ref[...] = _mm(hb, wkB_ref[...]).astype(BF16)
    vB_ref[...] = _nt(wvBT_ref[...], hb).astype(BF16)

    @pl.when(pl.program_id(1) == 0)
    def _():
        carry_ref[...] = jnp.zeros_like(carry_ref)

    cf = _mm(hb, wcf_ref[...]) + fb_ref[...]
    logf = -(jnp.maximum(-cf, 0.0) + jnp.log1p(jnp.exp(-jnp.abs(cf))))
    fl = _mm(tri_ref[...], jnp.concatenate(_split3(logf), axis=1))
    f_cum = fl[:, :HP] + fl[:, HP:2 * HP] + fl[:, 2 * HP:] + carry_ref[...]
    carry_ref[...] = f_cum[TM_IN - 1:TM_IN, :]
    fs = jnp.concatenate(_split3(f_cum * LOG2E), axis=1)
    kC_ref[...] = (_mm(hb, wkC_ref[...]) + _mm(fs, pk_ref[...]) + onesk_ref[...]).astype(BF16)
    qC_ref[...] = (_nt(wqCT_ref[...], hb) * bc_scale + _nt(pq_ref[...], fs)
                   + onesq_ref[...]).astype(BF16)
    vC_ref[...] = _nt(wvCT_ref[...], hb).astype(BF16)


def _const_spec(shape):
    nd = len(shape)
    return pl.BlockSpec(shape, lambda b, i, _nd=nd: (0,) * _nd)


def _in_call(x, mod_l, pos3, consts, lw):
    tok = lambda w: pl.BlockSpec((None, TM_IN, w), lambda b, i: (b, i, 0))
    fm = lambda r: pl.BlockSpec((None, r, TM_IN), lambda b, i: (b, 0, i))
    weights = [lw[k] for k in ("wcq", "wckv", "wkpeT", "wgT", "wqBT", "wkB", "wvBT",
                               "wqCT", "wkC", "wvCT", "wcf", "qng", "wuqT", "kvng",
                               "wuk", "wuvT", "fb")]
    cst = [consts[k] for k in ("tri", "pk", "pq", "onesk", "onesq")]
    in_specs = [
        tok(D_MODEL),
        pl.BlockSpec((None, None, 1, D_MODEL), lambda b, i: (0, b, 0, 0)),
        pl.BlockSpec((None, None, 1, D_MODEL), lambda b, i: (1, b, 0, 0)),
        _const_spec((1, D_MODEL)),
        pl.BlockSpec((None, 1, TM_IN), lambda b, i: (b, 0, i)),
        _const_spec((HALF, 1)),
    ] + [_const_spec(w.shape) for w in weights] + [_const_spec(c.shape) for c in cst]
    fm_shape = lambda r: jax.ShapeDtypeStruct((BATCH, r, SEQ), BF16)
    tok_shape = lambda w: jax.ShapeDtypeStruct((BATCH, SEQ, w), BF16)
    out_shape = [fm_shape(A_HEADS * HP), tok_shape(A_HEADS * HP), fm_shape(A_WIDTH),
                 fm_shape(B_HEADS * HP), tok_shape(B_HEADS * HP), fm_shape(B_WIDTH),
                 fm_shape(C_HEADS * HP), tok_shape(C_HEADS * HP), fm_shape(C_WIDTH),
                 fm_shape(D_MIX)]
    out_specs = [fm(A_HEADS * HP), tok(A_HEADS * HP), fm(A_WIDTH),
                 fm(B_HEADS * HP), tok(B_HEADS * HP), fm(B_WIDTH),
                 fm(C_HEADS * HP), tok(C_HEADS * HP), fm(C_WIDTH),
                 fm(D_MIX)]
    return pl.pallas_call(
        _in_kernel,
        grid=(BATCH, SEQ // TM_IN),
        in_specs=in_specs,
        out_specs=out_specs,
        out_shape=out_shape,
        scratch_shapes=[pltpu.VMEM((1, HP), F32)],
        compiler_params=pltpu.CompilerParams(
            dimension_semantics=("parallel", "arbitrary"),
            vmem_limit_bytes=V7X_VMEM_LIMIT_BYTES),
        name="in_proj",
    )(x, mod_l, mod_l, lw["norm_g"], pos3, consts["inv"], *weights, *cst)


def _attn_kernel(qT_ref, k_ref, vT_ref, gate_ref, bias_ref, o_ref, *, band):
    for i in range(NT):
        q0 = i * TQ
        qT = qT_ref[:, q0:q0 + TQ]
        if band is None:
            spans = ([(0, q0, None)] if i else []) + [(q0, q0 + TQ, bias_ref[...])]
        else:
            k0 = max(i - band, 0) * TQ
            spans = [(k0, q0 + TQ, bias_ref[(band + 1) * TQ - (q0 + TQ - k0):, :])]
        scores = []
        for a, b, bias in spans:
            s = _mm(k_ref[a:b, :], qT)
            scores.append(s if bias is None else s + bias)
        m = functools.reduce(jnp.maximum, [jnp.max(s, axis=0, keepdims=True) for s in scores])
        l = jnp.zeros((1, TQ), F32)
        acc = jnp.zeros((vT_ref.shape[0], TQ), F32)
        for (a, b, _), s in zip(spans, scores):
            p = jnp.exp2(s - m)
            l = l + jnp.sum(p, axis=0, keepdims=True)
            acc = acc + _mm(vT_ref[:, a:b], p.astype(BF16))
        o_ref[:, q0:q0 + TQ] = (acc * (1.0 / l) * gate_ref[:, q0:q0 + TQ].astype(F32)).astype(BF16)


def _attn_call(qT, k, vT, gateT, bias, *, heads, hd, gate_row0, band, name):
    if band is None:
        bias_spec = pl.BlockSpec(bias.shape, lambda b, h: (0, 0))
    else:
        bias_spec = pl.BlockSpec((None,) + bias.shape[1:], lambda b, h: (h, 0, 0))
    g0 = gate_row0 // hd
    return pl.pallas_call(
        functools.partial(_attn_kernel, band=band),
        grid=(BATCH, heads),
        in_specs=[
            pl.BlockSpec((None, HP, SEQ), lambda b, h: (b, h, 0)),
            pl.BlockSpec((None, SEQ, HP), lambda b, h: (b, 0, h)),
            pl.BlockSpec((None, hd, SEQ), lambda b, h: (b, h, 0)),
            pl.BlockSpec((None, hd, SEQ), lambda b, h: (b, g0 + h, 0)),
            bias_spec,
        ],
        out_specs=pl.BlockSpec((None, hd, SEQ), lambda b, h: (b, h, 0)),
        out_shape=jax.ShapeDtypeStruct((BATCH, heads * hd, SEQ), BF16),
        compiler_params=pltpu.CompilerParams(dimension_semantics=("parallel", "parallel")),
        name=name,
    )(qT, k, vT, gateT, bias)


def _bias_kernel(v_ref, mask_ref, o_ref):
    kk = lax.broadcasted_iota(jnp.int32, (TQ, 2 * TQ), 0)

    def toeplitz(row):
        x = jnp.broadcast_to(row, (TQ, 2 * TQ))
        for bit in range(TQ.bit_length() - 1):
            x = jnp.where(((kk >> bit) & 1) == 1, pltpu.roll(x, 1 << bit, axis=1), x)
        return x[:, TQ:]

    v0 = v_ref[0]
    v1 = v_ref[1]
    far = jnp.broadcast_to(v1[:, 2 * TQ - 1:], (TQ, TQ))
    for t, tile in enumerate((far, toeplitz(v1), toeplitz(v0))):
        rows = slice(t * TQ, (t + 1) * TQ)
        o_ref[rows, :] = tile * LOG2E + mask_ref[rows, :]


def _band_bias(rel_bias_l, mask_b):
    lo = jnp.repeat(rel_bias_l[:, :1], REL_CLIP, axis=1)
    hi = rel_bias_l[:, 2 * REL_CLIP:]
    v0 = jnp.concatenate([lo, rel_bias_l, jnp.repeat(hi, 2 * TQ - 3 * REL_CLIP - 1, axis=1)], axis=1)
    v1 = jnp.concatenate([rel_bias_l[:, REL_CLIP:], jnp.repeat(hi, 2 * TQ - REL_CLIP - 1, axis=1)], axis=1)
    v = jnp.stack([v0, v1], axis=1).reshape(B_HEADS, 2, 1, 2 * TQ)
    nrow = (B_BAND_TILES + 1) * TQ
    return pl.pallas_call(
        _bias_kernel,
        grid=(B_HEADS,),
        in_specs=[pl.BlockSpec((None, 2, 1, 2 * TQ), lambda h: (h, 0, 0, 0)),
                  pl.BlockSpec((nrow, TQ), lambda h: (0, 0))],
        out_specs=pl.BlockSpec((None, nrow, TQ), lambda h: (h, 0, 0)),
        out_shape=jax.ShapeDtypeStruct((B_HEADS, nrow, TQ), F32),
        name="band_bias",
    )(v, mask_b)


def _out_kernel(mA_ref, mB_ref, mC_ref, woT_ref, x_ref, gate_ref, fg_ref, o_ref, *, final):
    for t in range(TM_OUT // TQ):
        cols = slice(t * TQ, (t + 1) * TQ)
        mixT = jnp.concatenate([mA_ref[:, cols], mB_ref[:, cols], mC_ref[:, cols]], axis=0)
        y = _mm(woT_ref[...], mixT).T
        xn = x_ref[cols, :] + gate_ref[...] * y
        if final:
            xn = _rms(xn, fg_ref[...])
        o_ref[cols, :] = xn


def _out_call(mA, mB, mC, woT, x, mod_l, final_g, *, final):
    fm = lambda r: pl.BlockSpec((None, r, TM_OUT), lambda b, i: (b, 0, i))
    tok = pl.BlockSpec((None, TM_OUT, D_MODEL), lambda b, i: (b, i, 0))
    return pl.pallas_call(
        functools.partial(_out_kernel, final=final),
        grid=(BATCH, SEQ // TM_OUT),
        in_specs=[fm(A_WIDTH), fm(B_WIDTH), fm(C_WIDTH),
                  _const_spec((D_MODEL, D_MIX)), tok,
                  pl.BlockSpec((None, None, 1, D_MODEL), lambda b, i: (2, b, 0, 0)),
                  _const_spec((1, D_MODEL))],
        out_specs=tok,
        out_shape=jax.ShapeDtypeStruct((BATCH, SEQ, D_MODEL), F32),
        compiler_params=pltpu.CompilerParams(dimension_semantics=("parallel", "parallel")),
        name="out_proj",
    )(mA, mB, mC, woT, x, mod_l, final_g)


def _pad_heads(w, heads, hd):
    k = w.shape[0]
    return jnp.pad(w.reshape(k, heads, hd), ((0, 0), (0, 0), (0, HP - hd))).reshape(k, heads * HP)


def _constants():
    tri = np.tril(np.ones((TM_IN, TM_IN), np.float32))
    pk = np.zeros((3 * HP, C_HEADS * HP), np.float32)
    pq = np.zeros((C_HEADS * HP, 3 * HP), np.float32)
    onesk = np.zeros((1, C_HEADS * HP), np.float32)
    onesq = np.zeros((C_HEADS * HP, 1), np.float32)
    for hh in range(C_HEADS):
        for j in range(3):
            onesk[0, hh * HP + AUG + j] = 1.0
            pq[hh * HP + AUG + j, j * HP + hh] = 1.0
            pk[j * HP + hh, hh * HP + AUG + 3 + j] = -1.0
            onesq[hh * HP + AUG + 3 + j, 0] = 1.0
    kk = np.arange(TQ)[:, None]
    qq = np.arange(TQ)[None, :]
    mask_a = np.where(kk // CHUNK <= qq // CHUNK, 0.0, NEG).astype(np.float32)
    mask_c = np.where(kk <= qq, 0.0, NEG).astype(np.float32)
    d = np.arange(B_BAND_TILES, -1, -1)[:, None, None]
    cdist = (TQ // CHUNK) * d + qq[None] // CHUNK - kk[None] // CHUNK
    mask_b = np.where((cdist >= 0) & (cdist <= B_LEFT_CHUNKS), 0.0, NEG).astype(np.float32)
    inv = ROPE_THETA ** (-jnp.arange(0, A_ROPE, 2, dtype=F32) / A_ROPE)
    return {
        "tri": jnp.asarray(tri, BF16), "pk": jnp.asarray(pk, BF16), "pq": jnp.asarray(pq, BF16),
        "onesk": jnp.asarray(onesk), "onesq": jnp.asarray(onesq),
        "mask_a": jnp.asarray(mask_a), "mask_c": jnp.asarray(mask_c),
        "mask_b": jnp.asarray(mask_b.reshape(-1, TQ)), "inv": inv.reshape(HALF, 1),
    }


def _layer_weights(l, norm_g, w_in, a_q_norm_g, a_w_uq, a_kv_norm_g, a_w_ukv,
                   b_rel_bias, c_forget_b, w_out, consts):
    w = w_in[l]
    col = lambda n: w[:, _OFFS[n]:_OFFS[n + 1]]
    w_gate = jnp.concatenate([col(3), col(7), col(12)], axis=1)
    uq = jnp.pad(a_w_uq[l].reshape(A_Q_RANK, A_HEADS, A_NOPE + A_ROPE),
                 ((0, 0), (0, 0), (0, HP - A_NOPE - A_ROPE))).reshape(A_Q_RANK, A_HEADS * HP)
    ukv = a_w_ukv[l].reshape(A_KV_RANK, A_HEADS, A_NOPE + A_V)
    uk = jnp.pad(ukv[:, :, :A_NOPE], ((0, 0), (0, 0), (0, HP - A_NOPE))).reshape(A_KV_RANK, A_HEADS * HP)
    uv = ukv[:, :, A_NOPE:].reshape(A_KV_RANK, A_WIDTH)
    kpeT = jnp.pad(col(2).T, ((A_NOPE, HP - A_NOPE - A_ROPE), (0, 0)))
    bf = lambda a: a.astype(BF16)
    return {
        "norm_g": norm_g[l].reshape(1, D_MODEL),
        "wcq": bf(col(0)), "wckv": bf(col(1)), "wkpeT": bf(kpeT), "wgT": bf(w_gate.T),
        "wqBT": bf(_pad_heads(col(4), B_HEADS, B_HD).T),
        "wkB": bf(_pad_heads(col(5), B_HEADS, B_HD)), "wvBT": bf(col(6).T),
        "wqCT": bf(_pad_heads(col(8), C_HEADS, C_HD).T),
        "wkC": bf(_pad_heads(col(9), C_HEADS, C_HD)), "wvCT": bf(col(10).T),
        "wcf": bf(jnp.pad(col(11), ((0, 0), (0, HP - C_HEADS)))),
        "qng": a_q_norm_g[l].reshape(1, A_Q_RANK), "wuqT": bf(uq.T),
        "kvng": a_kv_norm_g[l].reshape(1, A_KV_RANK), "wuk": bf(uk), "wuvT": bf(uv.T),
        "fb": jnp.pad(c_forget_b[l], (0, HP - C_HEADS)).reshape(1, HP),
        "woT": bf(w_out[l].T), "bias_b": _band_bias(b_rel_bias[l], consts["mask_b"]),
    }


def kernel(x, c, positions, w_ada, b_ada, norm_g, w_in, a_q_norm_g, a_w_uq, a_kv_norm_g,
           a_w_ukv, b_rel_bias, c_forget_b, w_out, final_g):
    consts = _constants()
    mod = _modulation(c, w_ada, b_ada).reshape(DEPTH, 3, BATCH, 1, D_MODEL)
    pos3 = positions.reshape(BATCH, 1, SEQ)
    fg = final_g.reshape(1, D_MODEL)
    for l in range(DEPTH):
        lw = _layer_weights(l, norm_g, w_in, a_q_norm_g, a_w_uq, a_kv_norm_g, a_w_ukv,
                            b_rel_bias, c_forget_b, w_out, consts)
        qA, kA, vA, qB, kB, vB, qC, kC, vC, gateT = _in_call(x, mod[l], pos3, consts, lw)
        mA = _attn_call(qA, kA, vA, gateT, consts["mask_a"], heads=A_HEADS, hd=A_V,
                        gate_row0=0, band=None, name="attn_mla")
        mB = _attn_call(qB, kB, vB, gateT, lw["bias_b"], heads=B_HEADS, hd=B_HD,
                        gate_row0=A_WIDTH, band=B_BAND_TILES, name="attn_band")
        mC = _attn_call(qC, kC, vC, gateT, consts["mask_c"], heads=C_HEADS, hd=C_HD,
                        gate_row0=A_WIDTH + B_WIDTH, band=None, name="attn_forget")
        x = _out_call(mA, mB, mC, lw["woT"], x, mod[l], fg, final=(l == DEPTH - 1))
    return x
```

```python
import functools

import numpy as np
import jax
import jax.numpy as jnp
from jax import lax
from jax.experimental import pallas as pl
from jax.experimental.pallas import tpu as pltpu

F32 = jnp.float32
BF16 = jnp.bfloat16

D_MODEL = 1024
BATCH = 16
SEQ = 2048
DEPTH = 2
CHUNK = 64
EPS = 1e-6
NEG = -1e30

A_HEADS, A_NOPE, A_ROPE, A_V = 6, 64, 32, 64
A_Q_RANK, A_KV_RANK = 384, 256
A_WIDTH = A_HEADS * A_V
ROPE_THETA = 10000.0
B_HEADS, B_HD = 5, 64
B_WIDTH = B_HEADS * B_HD
B_LEFT_CHUNKS = 8
REL_CLIP = 128
C_HEADS, C_HD = 5, 64
C_WIDTH = C_HEADS * C_HD
D_MIX = A_WIDTH + B_WIDTH + C_WIDTH

_SIZES = (A_Q_RANK, A_KV_RANK, A_ROPE, A_WIDTH, B_WIDTH, B_WIDTH, B_WIDTH, B_WIDTH,
          C_WIDTH, C_WIDTH, C_WIDTH, C_HEADS, C_WIDTH)
_OFFS = tuple(int(v) for v in np.cumsum((0,) + _SIZES))

LOG2E = 1.4426950408889634
HP = 128
TQ = 256
TK = 256
NT = SEQ // TQ
TM_IN = 512
TM_OUT = 512
HALF = A_ROPE // 2
AUG = HP // 2
B_BAND_TILES = (B_LEFT_CHUNKS * CHUNK) // TQ

V7X_VMEM_LIMIT_BYTES = 56 * 1024 * 1024


def _mm(a, b):
    return jnp.dot(a, b, preferred_element_type=F32)


def _nt(a, b):
    return lax.dot_general(a, b, (((1,), (1,)), ((), ())), preferred_element_type=F32)


def _rms(x, g):
    return x * lax.rsqrt(jnp.mean(x * x, axis=-1, keepdims=True) + EPS) * g


def _silu(x):
    return x / (1.0 + jnp.exp(-x))


def _split3(x):
    p1 = x.astype(BF16)
    r1 = x - p1.astype(F32)
    p2 = r1.astype(BF16)
    p3 = (r1 - p2.astype(F32)).astype(BF16)
    return p1, p2, p3


def _mod_kernel(c_ref, w_ref, b_ref, o_ref):
    ca = _silu(c_ref[...])
    w = w_ref[...]
    ch = ca.astype(BF16)
    cl = (ca - ch.astype(F32)).astype(BF16)
    wh = w.astype(BF16)
    wl = (w - wh.astype(F32)).astype(BF16)
    o_ref[...] = _mm(ch, wh) + _mm(ch, wl) + _mm(cl, wh) + b_ref[...]


def _modulation(c, w_ada, b_ada):
    return pl.pallas_call(
        _mod_kernel,
        grid=(DEPTH, 3),
        in_specs=[
            pl.BlockSpec((BATCH, D_MODEL), lambda l, j: (0, 0)),
            pl.BlockSpec((None, D_MODEL, D_MODEL), lambda l, j: (l, 0, j)),
            pl.BlockSpec((None, None, 1, D_MODEL), lambda l, j: (l, j, 0, 0)),
        ],
        out_specs=pl.BlockSpec((None, None, BATCH, D_MODEL), lambda l, j: (l, j, 0, 0)),
        out_shape=jax.ShapeDtypeStruct((DEPTH, 3, BATCH, D_MODEL), F32),
        name="adaln_mod",
    )(c, w_ada, b_ada.reshape(DEPTH, 3, 1, D_MODEL))


def _in_kernel(x_ref, shift_ref, scale_ref, g_ref, pos_ref, inv_ref,
               wcq_ref, wckv_ref, wkpeT_ref, wgT_ref,
               wqBT_ref, wkB_ref, wvBT_ref, wqCT_ref, wkC_ref, wvCT_ref, wcf_ref,
               qng_ref, wuqT_ref, kvng_ref, wuk_ref, wuvT_ref,
               fb_ref, tri_ref, pk_ref, pq_ref, onesk_ref, onesq_ref,
               qA_ref, kA_ref, vA_ref, qB_ref, kB_ref, vB_ref, qC_ref, kC_ref, vC_ref,
               gate_ref, carry_ref):
    x = x_ref[...]
    h = _rms(x, g_ref[...]) * (1.0 + scale_ref[...]) + shift_ref[...]
    hb = h.astype(BF16)

    for r in range(0, D_MIX, 256):
        gate_ref[r:r + 256, :] = _silu(_nt(wgT_ref[r:r + 256, :], hb)).astype(BF16)

    ang = inv_ref[...] * pos_ref[...].astype(F32)
    cos = jnp.cos(ang)
    sin = jnp.sin(ang)

    cqn = _rms(_mm(hb, wcq_ref[...]), qng_ref[...]).astype(BF16)
    a_scale = (A_NOPE + A_ROPE) ** -0.5 * LOG2E
    for hh in range(A_HEADS):
        base = hh * HP
        qh = _nt(wuqT_ref[base:base + HP, :], cqn) * a_scale
        x1 = qh[A_NOPE:A_NOPE + HALF]
        x2 = qh[A_NOPE + HALF:A_NOPE + A_ROPE]
        qA_ref[base:base + HP, :] = jnp.concatenate(
            [qh[:A_NOPE], x1 * cos - x2 * sin, x1 * sin + x2 * cos,
             jnp.zeros((HP - A_NOPE - A_ROPE, TM_IN), F32)], axis=0).astype(BF16)

    ckvn = _rms(_mm(hb, wckv_ref[...]), kvng_ref[...]).astype(BF16)
    vA_ref[...] = _nt(wuvT_ref[...], ckvn).astype(BF16)
    kpeT = _nt(wkpeT_ref[...], hb)
    x1 = kpeT[A_NOPE:A_NOPE + HALF]
    x2 = kpeT[A_NOPE + HALF:A_NOPE + A_ROPE]
    kpe_tok = jnp.concatenate(
        [jnp.zeros((A_NOPE, TM_IN), F32), x1 * cos - x2 * sin, x1 * sin + x2 * cos,
         jnp.zeros((HP - A_NOPE - A_ROPE, TM_IN), F32)], axis=0).T
    kn = _mm(ckvn, wuk_ref[...])
    for hh in range(A_HEADS):
        sl = slice(hh * HP, (hh + 1) * HP)
        kA_ref[:, sl] = (kn[:, sl] + kpe_tok).astype(BF16)

    bc_scale = B_HD ** -0.5 * LOG2E
    qB_ref[...] = (_nt(wqBT_ref[...], hb) * bc_scale).astype(BF16)
    kB_ref[...] = _mm(hb, wkB_ref[...]).astype(BF16)
    vB_ref[...] = _nt(wvBT_ref[...], hb).astype(BF16)

    @pl.when(pl.program_id(1) == 0)
    def _():
        carry_ref[...] = jnp.zeros_like(carry_ref)

    cf = _mm(hb, wcf_ref[...]) + fb_ref[...]
    logf = -(jnp.maximum(-cf, 0.0) + jnp.log1p(jnp.exp(-jnp.abs(cf))))
    fl = _mm(tri_ref[...], jnp.concatenate(_split3(logf), axis=1))
    f_cum = fl[:, :HP] + fl[:, HP:2 * HP] + fl[:, 2 * HP:] + carry_ref[...]
    carry_ref[...] = f_cum[TM_IN - 1:TM_IN, :]
    fs = jnp.concatenate(_split3(f_cum * LOG2E), axis=1)
    kC_ref[...] = (_mm(hb, wkC_ref[...]) + _mm(fs, pk_ref[...]) + onesk_ref[...]).astype(BF16)
    qC_ref[...] = (_nt(wqCT_ref[...], hb) * bc_scale + _nt(pq_ref[...], fs)
                   + onesq_ref[...]).astype(BF16)
    vC_ref[...] = _nt(wvCT_ref[...], hb).astype(BF16)


def _const_spec(shape, single_buffer=False):
    nd = len(shape)
    mode = pl.Buffered(1) if single_buffer else None
    return pl.BlockSpec(shape, lambda b, i, _nd=nd: (0,) * _nd, pipeline_mode=mode)


def _in_call(x, mod_l, pos3, consts, lw):
    tok = lambda w: pl.BlockSpec((None, TM_IN, w), lambda b, i: (b, i, 0))
    fm = lambda r: pl.BlockSpec((None, r, TM_IN), lambda b, i: (b, 0, i))
    weights = [lw[k] for k in ("wcq", "wckv", "wkpeT", "wgT", "wqBT", "wkB", "wvBT",
                               "wqCT", "wkC", "wvCT", "wcf", "qng", "wuqT", "kvng",
                               "wuk", "wuvT", "fb")]
    cst = [consts[k] for k in ("tri", "pk", "pq", "onesk", "onesq")]
    in_specs = [
        tok(D_MODEL),
        pl.BlockSpec((None, None, 1, D_MODEL), lambda b, i: (0, b, 0, 0)),
        pl.BlockSpec((None, None, 1, D_MODEL), lambda b, i: (1, b, 0, 0)),
        _const_spec((1, D_MODEL)),
        pl.BlockSpec((None, 1, TM_IN), lambda b, i: (b, 0, i)),
        _const_spec((HALF, 1)),
    ] + [_const_spec(w.shape, True) for w in weights] + [_const_spec(c.shape, True) for c in cst]
    fm_shape = lambda r: jax.ShapeDtypeStruct((BATCH, r, SEQ), BF16)
    tok_shape = lambda w: jax.ShapeDtypeStruct((BATCH, SEQ, w), BF16)
    out_shape = [fm_shape(A_HEADS * HP), tok_shape(A_HEADS * HP), fm_shape(A_WIDTH),
                 fm_shape(B_HEADS * HP), tok_shape(B_HEADS * HP), fm_shape(B_WIDTH),
                 fm_shape(C_HEADS * HP), tok_shape(C_HEADS * HP), fm_shape(C_WIDTH),
                 fm_shape(D_MIX)]
    out_specs = [fm(A_HEADS * HP), tok(A_HEADS * HP), fm(A_WIDTH),
                 fm(B_HEADS * HP), tok(B_HEADS * HP), fm(B_WIDTH),
                 fm(C_HEADS * HP), tok(C_HEADS * HP), fm(C_WIDTH),
                 fm(D_MIX)]
    return pl.pallas_call(
        _in_kernel,
        grid=(BATCH, SEQ // TM_IN),
        in_specs=in_specs,
        out_specs=out_specs,
        out_shape=out_shape,
        scratch_shapes=[pltpu.VMEM((1, HP), F32)],
        compiler_params=pltpu.CompilerParams(
            dimension_semantics=("parallel", "arbitrary"),
            vmem_limit_bytes=V7X_VMEM_LIMIT_BYTES),
        name="in_proj",
    )(x, mod_l, mod_l, lw["norm_g"], pos3, consts["inv"], *weights, *cst)


def _attn_kernel(qT_ref, k_ref, vT_ref, gate_ref, bias_ref, o_ref,
                 s0_ref, s1_ref, p0_ref, p1_ref, *, band):
    s_bufs = (s0_ref, s1_ref)
    p_bufs = (p0_ref, p1_ref)
    hd = vT_ref.shape[0]
    tile = lambda t: slice(t * TQ, (t + 1) * TQ)
    ktile = lambda c: slice(c * TK, (c + 1) * TK)

    def key_tiles(i):
        first = 0 if band is None else max(i - band, 0) * TQ // TK
        return range(first, (i + 1) * TQ // TK)

    def tile_bias(i, c):
        off = c * TK - i * TQ + (0 if band is None else band * TQ)
        return bias_ref[off:off + TK, :] if off >= 0 else None

    def score_steps(i, state):
        qT = qT_ref[:, tile(i)]

        def step(c):
            s = _mm(k_ref[ktile(c), :], qT)
            bias = tile_bias(i, c)
            if bias is not None:
                s = s + bias
            s_bufs[slot[i]][c] = s
            pm = jnp.max(s.reshape(TK // 8, 8, TQ), axis=0)
            state["mx"] = pm if state["mx"] is None else jnp.maximum(state["mx"], pm)

        return [functools.partial(step, c) for c in key_tiles(i)]

    def exp_steps(i, state):
        m = jnp.max(state["mx"], axis=0, keepdims=True)
        state["l"] = jnp.zeros((8, TQ), F32)

        def step(c):
            p = jnp.exp2(s_bufs[slot[i]][c] - m)
            state["l"] = state["l"] + jnp.sum(p.reshape(TK // 8, 8, TQ), axis=0)
            p_bufs[slot[i]][c] = p.astype(BF16)

        return [functools.partial(step, c) for c in key_tiles(i)]

    def pv_steps(i, state):
        state["acc"] = jnp.zeros((hd, TQ), F32)

        def step(c):
            state["acc"] = state["acc"] + _mm(vT_ref[:, ktile(c)], p_bufs[slot[i]][c])

        return [functools.partial(step, c) for c in key_tiles(i)]

    def finish(i, state):
        l = jnp.sum(state["l"], axis=0, keepdims=True)
        o_ref[:, tile(i)] = (state["acc"] * (1.0 / l)
                             * gate_ref[:, tile(i)].astype(F32)).astype(BF16)

    order = []
    lo, hi = 1, NT - 1
    while lo <= hi:
        order += [lo, hi] if lo < hi else [lo]
        lo, hi = lo + 1, hi - 1
    order.append(0)
    slot = {q: n % 2 for n, q in enumerate(order)}
    states = {q: {"mx": None} for q in order}
    for t in range(NT + 2):
        streams = []
        if t < NT:
            streams.append(score_steps(order[t], states[order[t]]))
        if 0 <= t - 1 < NT:
            streams.append(exp_steps(order[t - 1], states[order[t - 1]]))
        if 0 <= t - 2 < NT:
            streams.append(pv_steps(order[t - 2], states[order[t - 2]]))
        for n in range(max(len(s) for s in streams)):
            for s in streams:
                if n < len(s):
                    s[n]()
        if 0 <= t - 2 < NT:
            finish(order[t - 2], states[order[t - 2]])


def _attn_call(qT, k, vT, gateT, bias, *, heads, hd, gate_row0, band, name):
    if band is None:
        bias_spec = pl.BlockSpec(bias.shape, lambda b, h: (0, 0))
    else:
        bias_spec = pl.BlockSpec((None,) + bias.shape[1:], lambda b, h: (h, 0, 0))
    g0 = gate_row0 // hd
    return pl.pallas_call(
        functools.partial(_attn_kernel, band=band),
        grid=(BATCH, heads),
        in_specs=[
            pl.BlockSpec((None, HP, SEQ), lambda b, h: (b, h, 0)),
            pl.BlockSpec((None, SEQ, HP), lambda b, h: (b, 0, h)),
            pl.BlockSpec((None, hd, SEQ), lambda b, h: (b, h, 0)),
            pl.BlockSpec((None, hd, SEQ), lambda b, h: (b, g0 + h, 0)),
            bias_spec,
        ],
        out_specs=pl.BlockSpec((None, hd, SEQ), lambda b, h: (b, h, 0)),
        out_shape=jax.ShapeDtypeStruct((BATCH, heads * hd, SEQ), BF16),
        scratch_shapes=([pltpu.VMEM((SEQ // TK, TK, TQ), F32)] * 2
                        + [pltpu.VMEM((SEQ // TK, TK, TQ), BF16)] * 2),
        compiler_params=pltpu.CompilerParams(dimension_semantics=("parallel", "parallel")),
        name=name,
    )(qT, k, vT, gateT, bias)


def _bias_kernel(v_ref, mask_ref, o_ref):
    kk = lax.broadcasted_iota(jnp.int32, (TQ, 2 * TQ), 0)

    def toeplitz(row):
        x = jnp.broadcast_to(row, (TQ, 2 * TQ))
        for bit in range(TQ.bit_length() - 1):
            x = jnp.where(((kk >> bit) & 1) == 1, pltpu.roll(x, 1 << bit, axis=1), x)
        return x[:, TQ:]

    v0 = v_ref[0]
    v1 = v_ref[1]
    far = jnp.broadcast_to(v1[:, 2 * TQ - 1:], (TQ, TQ))
    for t, tile in enumerate((far, toeplitz(v1), toeplitz(v0))):
        rows = slice(t * TQ, (t + 1) * TQ)
        o_ref[rows, :] = tile * LOG2E + mask_ref[rows, :]


def _band_bias(rel_bias_l, mask_b):
    lo = jnp.repeat(rel_bias_l[:, :1], REL_CLIP, axis=1)
    hi = rel_bias_l[:, 2 * REL_CLIP:]
    v0 = jnp.concatenate([lo, rel_bias_l, jnp.repeat(hi, 2 * TQ - 3 * REL_CLIP - 1, axis=1)], axis=1)
    v1 = jnp.concatenate([rel_bias_l[:, REL_CLIP:], jnp.repeat(hi, 2 * TQ - REL_CLIP - 1, axis=1)], axis=1)
    v = jnp.stack([v0, v1], axis=1).reshape(B_HEADS, 2, 1, 2 * TQ)
    nrow = (B_BAND_TILES + 1) * TQ
    return pl.pallas_call(
        _bias_kernel,
        grid=(B_HEADS,),
        in_specs=[pl.BlockSpec((None, 2, 1, 2 * TQ), lambda h: (h, 0, 0, 0)),
                  pl.BlockSpec((nrow, TQ), lambda h: (0, 0))],
        out_specs=pl.BlockSpec((None, nrow, TQ), lambda h: (h, 0, 0)),
        out_shape=jax.ShapeDtypeStruct((B_HEADS, nrow, TQ), F32),
        name="band_bias",
    )(v, mask_b)


def _out_kernel(mA_ref, mB_ref, mC_ref, woT_ref, x_ref, gate_ref, fg_ref, o_ref, *, final):
    for t in range(TM_OUT // TQ):
        cols = slice(t * TQ, (t + 1) * TQ)
        mixT = jnp.concatenate([mA_ref[:, cols], mB_ref[:, cols], mC_ref[:, cols]], axis=0)
        y = _mm(woT_ref[...], mixT).T
        xn = x_ref[cols, :] + gate_ref[...] * y
        if final:
            xn = _rms(xn, fg_ref[...])
        o_ref[cols, :] = xn


def _out_call(mA, mB, mC, woT, x, mod_l, final_g, *, final):
    fm = lambda r: pl.BlockSpec((None, r, TM_OUT), lambda b, i: (b, 0, i))
    tok = pl.BlockSpec((None, TM_OUT, D_MODEL), lambda b, i: (b, i, 0))
    return pl.pallas_call(
        functools.partial(_out_kernel, final=final),
        grid=(BATCH, SEQ // TM_OUT),
        in_specs=[fm(A_WIDTH), fm(B_WIDTH), fm(C_WIDTH),
                  _const_spec((D_MODEL, D_MIX)), tok,
                  pl.BlockSpec((None, None, 1, D_MODEL), lambda b, i: (2, b, 0, 0)),
                  _const_spec((1, D_MODEL))],
        out_specs=tok,
        out_shape=jax.ShapeDtypeStruct((BATCH, SEQ, D_MODEL), F32),
        compiler_params=pltpu.CompilerParams(dimension_semantics=("parallel", "parallel")),
        name="out_proj",
    )(mA, mB, mC, woT, x, mod_l, final_g)


def _transpose_kernel(w_ref, o_ref):
    o_ref[...] = w_ref[...].T.astype(BF16)


def _transpose_bf16(w):
    k, n = w.shape
    bn = 2 * HP if n % (2 * HP) == 0 else HP
    return pl.pallas_call(
        _transpose_kernel,
        grid=(n // bn,),
        in_specs=[pl.BlockSpec((k, bn), lambda j: (0, j))],
        out_specs=pl.BlockSpec((bn, k), lambda j: (j, 0)),
        out_shape=jax.ShapeDtypeStruct((n, k), BF16),
        name="w_transpose",
    )(w)


def _pad_heads(w, heads, hd):
    k = w.shape[0]
    return jnp.pad(w.reshape(k, heads, hd), ((0, 0), (0, 0), (0, HP - hd))).reshape(k, heads * HP)


def _constants():
    tri = np.tril(np.ones((TM_IN, TM_IN), np.float32))
    pk = np.zeros((3 * HP, C_HEADS * HP), np.float32)
    pq = np.zeros((C_HEADS * HP, 3 * HP), np.float32)
    onesk = np.zeros((1, C_HEADS * HP), np.float32)
    onesq = np.zeros((C_HEADS * HP, 1), np.float32)
    for hh in range(C_HEADS):
        for j in range(3):
            onesk[0, hh * HP + AUG + j] = 1.0
            pq[hh * HP + AUG + j, j * HP + hh] = 1.0
            pk[j * HP + hh, hh * HP + AUG + 3 + j] = -1.0
            onesq[hh * HP + AUG + 3 + j, 0] = 1.0
    kk = np.arange(TQ)[:, None]
    qq = np.arange(TQ)[None, :]
    mask_a = np.where(kk // CHUNK <= qq // CHUNK, 0.0, NEG).astype(np.float32)
    mask_c = np.where(kk <= qq, 0.0, NEG).astype(np.float32)
    d = np.arange(B_BAND_TILES, -1, -1)[:, None, None]
    cdist = (TQ // CHUNK) * d + qq[None] // CHUNK - kk[None] // CHUNK
    mask_b = np.where((cdist >= 0) & (cdist <= B_LEFT_CHUNKS), 0.0, NEG).astype(np.float32)
    inv = ROPE_THETA ** (-jnp.arange(0, A_ROPE, 2, dtype=F32) / A_ROPE)
    return {
        "tri": jnp.asarray(tri, BF16), "pk": jnp.asarray(pk, BF16), "pq": jnp.asarray(pq, BF16),
        "onesk": jnp.asarray(onesk), "onesq": jnp.asarray(onesq),
        "mask_a": jnp.asarray(mask_a), "mask_c": jnp.asarray(mask_c),
        "mask_b": jnp.asarray(mask_b.reshape(-1, TQ)), "inv": inv.reshape(HALF, 1),
    }


def _layer_weights(l, norm_g, w_in, a_q_norm_g, a_w_uq, a_kv_norm_g, a_w_ukv,
                   b_rel_bias, c_forget_b, w_out, consts):
    w = w_in[l]
    col = lambda n: w[:, _OFFS[n]:_OFFS[n + 1]]
    w_gate = jnp.concatenate([col(3), col(7), col(12)], axis=1)
    uq = jnp.pad(a_w_uq[l].reshape(A_Q_RANK, A_HEADS, A_NOPE + A_ROPE),
                 ((0, 0), (0, 0), (0, HP - A_NOPE - A_ROPE))).reshape(A_Q_RANK, A_HEADS * HP)
    ukv = a_w_ukv[l].reshape(A_KV_RANK, A_HEADS, A_NOPE + A_V)
    uk = jnp.pad(ukv[:, :, :A_NOPE], ((0, 0), (0, 0), (0, HP - A_NOPE))).reshape(A_KV_RANK, A_HEADS * HP)
    uv = ukv[:, :, A_NOPE:].reshape(A_KV_RANK, A_WIDTH)
    kpe = jnp.pad(col(2), ((0, 0), (A_NOPE, HP - A_NOPE - A_ROPE)))
    feat = [w_gate, _pad_heads(col(4), B_HEADS, B_HD), col(6),
            _pad_heads(col(8), C_HEADS, C_HD), col(10), kpe]
    featT = _transpose_bf16(jnp.concatenate(feat, axis=1))
    rows = np.cumsum([0] + [f.shape[1] for f in feat])
    wgT, wqBT, wvBT, wqCT, wvCT, wkpeT = (featT[a:b] for a, b in zip(rows[:-1], rows[1:]))
    bf = lambda a: a.astype(BF16)
    return {
        "norm_g": norm_g[l].reshape(1, D_MODEL),
        "wcq": bf(col(0)), "wckv": bf(col(1)), "wkpeT": wkpeT, "wgT": wgT,
        "wqBT": wqBT, "wkB": bf(_pad_heads(col(5), B_HEADS, B_HD)), "wvBT": wvBT,
        "wqCT": wqCT, "wkC": bf(_pad_heads(col(9), C_HEADS, C_HD)), "wvCT": wvCT,
        "wcf": bf(jnp.pad(col(11), ((0, 0), (0, HP - C_HEADS)))),
        "qng": a_q_norm_g[l].reshape(1, A_Q_RANK), "wuqT": _transpose_bf16(uq),
        "kvng": a_kv_norm_g[l].reshape(1, A_KV_RANK), "wuk": bf(uk),
        "wuvT": _transpose_bf16(uv),
        "fb": jnp.pad(c_forget_b[l], (0, HP - C_HEADS)).reshape(1, HP),
        "woT": _transpose_bf16(w_out[l]),
        "bias_b": _band_bias(b_rel_bias[l], consts["mask_b"]),
    }


def kernel(x, c, positions, w_ada, b_ada, norm_g, w_in, a_q_norm_g, a_w_uq, a_kv_norm_g,
           a_w_ukv, b_rel_bias, c_forget_b, w_out, final_g):
    consts = _constants()
    mod = _modulation(c, w_ada, b_ada).reshape(DEPTH, 3, BATCH, 1, D_MODEL)
    pos3 = positions.reshape(BATCH, 1, SEQ)
    fg = final_g.reshape(1, D_MODEL)
    for l in range(DEPTH):
        lw = _layer_weights(l, norm_g, w_in, a_q_norm_g, a_w_uq, a_kv_norm_g, a_w_ukv,
                            b_rel_bias, c_forget_b, w_out, consts)
        qA, kA, vA, qB, kB, vB, qC, kC, vC, gateT = _in_call(x, mod[l], pos3, consts, lw)
        mA = _attn_call(qA, kA, vA, gateT, consts["mask_a"], heads=A_HEADS, hd=A_V,
                        gate_row0=0, band=None, name="attn_mla")
        mB = _attn_call(qB, kB, vB, gateT, lw["bias_b"], heads=B_HEADS, hd=B_HD,
                        gate_row0=A_WIDTH, band=B_BAND_TILES, name="attn_band")
        mC = _attn_call(qC, kC, vC, gateT, consts["mask_c"], heads=C_HEADS, hd=C_HD,
                        gate_row0=A_WIDTH + B_WIDTH, band=None, name="attn_forget")
        x = _out_call(mA, mB, mC, lw["woT"], x, mod[l], fg, final=(l == DEPTH - 1))
    return x
```

```python
import functools

import numpy as np
import jax
import jax.numpy as jnp
from jax import lax
from jax.experimental import pallas as pl
from jax.experimental.pallas import tpu as pltpu

F32 = jnp.float32
BF16 = jnp.bfloat16

D_MODEL = 1024
BATCH = 16
SEQ = 2048
DEPTH = 2
CHUNK = 64
EPS = 1e-6
NEG = -1e30

A_HEADS, A_NOPE, A_ROPE, A_V = 6, 64, 32, 64
A_Q_RANK, A_KV_RANK = 384, 256
A_WIDTH = A_HEADS * A_V
ROPE_THETA = 10000.0
B_HEADS, B_HD = 5, 64
B_WIDTH = B_HEADS * B_HD
B_LEFT_CHUNKS = 8
REL_CLIP = 128
C_HEADS, C_HD = 5, 64
C_WIDTH = C_HEADS * C_HD
D_MIX = A_WIDTH + B_WIDTH + C_WIDTH

_SIZES = (A_Q_RANK, A_KV_RANK, A_ROPE, A_WIDTH, B_WIDTH, B_WIDTH, B_WIDTH, B_WIDTH,
          C_WIDTH, C_WIDTH, C_WIDTH, C_HEADS, C_WIDTH)
_OFFS = tuple(int(v) for v in np.cumsum((0,) + _SIZES))

LOG2E = 1.4426950408889634
HP = 128
PAIRS = 3
PW = PAIRS * HP
TQ = 256
TK = 256
NT = SEQ // TQ
TM_IN = 512
TM_OUT = 512
HALF = A_ROPE // 2
SLOT = 8
B_BAND_TILES = (B_LEFT_CHUNKS * CHUNK) // TQ

V7X_VMEM_LIMIT_BYTES = 56 * 1024 * 1024

_FEAT = (("gate", D_MIX), ("qB", PW), ("vB", B_WIDTH), ("qC", PW), ("vC", C_WIDTH), ("kpe", HP))
_FEAT_OFF = dict(zip([n for n, _ in _FEAT], np.cumsum([0] + [r for _, r in _FEAT])[:-1].tolist()))
_FEAT_ROWS = dict(_FEAT)
_TOK = (("cq", A_Q_RANK), ("ckv", A_KV_RANK), ("kB", PW), ("kC", PW), ("cf", HP))
_TOK_OFF = dict(zip([n for n, _ in _TOK], np.cumsum([0] + [r for _, r in _TOK])[:-1].tolist()))
_TOK_COLS = dict(_TOK)


def _mm(a, b):
    return jnp.dot(a, b, preferred_element_type=F32)


def _nt(a, b):
    return lax.dot_general(a, b, (((1,), (1,)), ((), ())), preferred_element_type=F32)


def _rms(x, g):
    return x * lax.rsqrt(jnp.mean(x * x, axis=-1, keepdims=True) + EPS) * g


def _silu(x):
    return x / (1.0 + jnp.exp(-x))


def _split3(x):
    p1 = x.astype(BF16).astype(F32)
    r1 = x - p1
    p2 = r1.astype(BF16).astype(F32)
    p3 = (r1 - p2).astype(BF16).astype(F32)
    return p1, p2, p3


def _mod_kernel(c_ref, w_ref, b_ref, o_ref):
    ca = _silu(c_ref[...])
    w = w_ref[...]
    ch = ca.astype(BF16)
    cl = (ca - ch.astype(F32)).astype(BF16)
    wh = w.astype(BF16)
    wl = (w - wh.astype(F32)).astype(BF16)
    o_ref[...] = _mm(ch, wh) + _mm(ch, wl) + _mm(cl, wh) + b_ref[...]


def _modulation(c, w_ada, b_ada):
    return pl.pallas_call(
        _mod_kernel,
        grid=(DEPTH, 3),
        in_specs=[
            pl.BlockSpec((BATCH, D_MODEL), lambda l, j: (0, 0)),
            pl.BlockSpec((None, D_MODEL, D_MODEL), lambda l, j: (l, 0, j)),
            pl.BlockSpec((None, None, 1, D_MODEL), lambda l, j: (l, j, 0, 0)),
        ],
        out_specs=pl.BlockSpec((None, None, BATCH, D_MODEL), lambda l, j: (l, j, 0, 0)),
        out_shape=jax.ShapeDtypeStruct((DEPTH, 3, BATCH, D_MODEL), F32),
        name="adaln_mod",
    )(c, w_ada, b_ada.reshape(DEPTH, 3, 1, D_MODEL))


def _in_kernel(x_ref, shift_ref, scale_ref, g_ref, pos_ref, inv_ref,
               wtok_ref, wfeatT_ref, qng_ref, wuqT_ref, kvng_ref, wuk_ref, wuvT_ref,
               fb_ref, tri_ref, pk_ref, pq_ref, onesk_ref, onesq_ref,
               qA_ref, kA_ref, vA_ref, qB_ref, kB_ref, vB_ref, qC_ref, kC_ref, vC_ref,
               kaug_ref, qaug_ref, gate_ref, carry_ref):
    x = x_ref[...]
    h = _rms(x, g_ref[...]) * (1.0 + scale_ref[...]) + shift_ref[...]
    hb = h.astype(BF16)

    def feat(name, r0=0, r1=None):
        base = _FEAT_OFF[name]
        r1 = _FEAT_ROWS[name] if r1 is None else r1
        return _nt(wfeatT_ref[base + r0:base + r1, :], hb)

    ztok = _mm(hb, wtok_ref[...])
    tok = lambda name: ztok[:, _TOK_OFF[name]:_TOK_OFF[name] + _TOK_COLS[name]]

    for r in range(0, D_MIX, 256):
        gate_ref[r:r + 256, :] = _silu(feat("gate", r, r + 256)).astype(BF16)

    ang = inv_ref[...] * pos_ref[...].astype(F32)
    cos = jnp.cos(ang)
    sin = jnp.sin(ang)

    cqn = _rms(tok("cq"), qng_ref[...]).astype(BF16)
    a_scale = (A_NOPE + A_ROPE) ** -0.5 * LOG2E
    for hh in range(A_HEADS):
        base = hh * HP
        qh = _nt(wuqT_ref[base:base + HP, :], cqn) * a_scale
        x1 = qh[A_NOPE:A_NOPE + HALF]
        x2 = qh[A_NOPE + HALF:A_NOPE + A_ROPE]
        qA_ref[base:base + HP, :] = jnp.concatenate(
            [qh[:A_NOPE], x1 * cos - x2 * sin, x1 * sin + x2 * cos,
             jnp.zeros((HP - A_NOPE - A_ROPE, TM_IN), F32)], axis=0).astype(BF16)

    ckvn = _rms(tok("ckv"), kvng_ref[...]).astype(BF16)
    vA_ref[...] = _nt(wuvT_ref[...], ckvn).astype(BF16)
    kpeT = feat("kpe")
    x1 = kpeT[A_NOPE:A_NOPE + HALF]
    x2 = kpeT[A_NOPE + HALF:A_NOPE + A_ROPE]
    kpe_tok = jnp.concatenate(
        [jnp.zeros((A_NOPE, TM_IN), F32), x1 * cos - x2 * sin, x1 * sin + x2 * cos,
         jnp.zeros((HP - A_NOPE - A_ROPE, TM_IN), F32)], axis=0).T
    kn = _mm(ckvn, wuk_ref[...])
    for hh in range(A_HEADS):
        sl = slice(hh * HP, (hh + 1) * HP)
        kA_ref[:, sl] = (kn[:, sl] + kpe_tok).astype(BF16)

    bc_scale = B_HD ** -0.5 * LOG2E
    qB_ref[...] = (feat("qB") * bc_scale).astype(BF16)
    kB_ref[...] = tok("kB").astype(BF16)
    vB_ref[...] = feat("vB").astype(BF16)
    qC_ref[...] = (feat("qC") * bc_scale).astype(BF16)
    kC_ref[...] = tok("kC").astype(BF16)
    vC_ref[...] = feat("vC").astype(BF16)

    @pl.when(pl.program_id(1) == 0)
    def _():
        carry_ref[...] = jnp.zeros_like(carry_ref)

    lane = lax.broadcasted_iota(jnp.int32, (TM_IN, HP), 1)
    head_lane = lane < C_HEADS

    def pack3(v):
        p1, p2, p3 = (jnp.where(head_lane, p, 0.0) for p in _split3(v))
        return (p1 + pltpu.roll(p2, SLOT, axis=1) + pltpu.roll(p3, 2 * SLOT, axis=1)).astype(BF16)

    cf = tok("cf") + fb_ref[...]
    logf = -(jnp.maximum(-cf, 0.0) + jnp.log1p(jnp.exp(-jnp.abs(cf))))
    fl = _mm(tri_ref[...], pack3(logf))
    f_cum = (fl + pltpu.roll(fl, HP - SLOT, axis=1) + pltpu.roll(fl, HP - 2 * SLOT, axis=1)
             + carry_ref[...])
    carry_ref[...] = f_cum[TM_IN - 1:TM_IN, :]
    fs = pack3(f_cum * LOG2E)
    kaug_ref[...] = (_mm(fs, pk_ref[...]) + onesk_ref[...]).astype(BF16)
    qaug_ref[...] = (_nt(pq_ref[...], fs) + onesq_ref[...]).astype(BF16)


def _const_spec(shape, single_buffer=False):
    nd = len(shape)
    mode = pl.Buffered(1) if single_buffer else None
    return pl.BlockSpec(shape, lambda b, i, _nd=nd: (0,) * _nd, pipeline_mode=mode)


def _in_call(x, mod_l, pos3, consts, lw):
    tok = lambda w: pl.BlockSpec((None, TM_IN, w), lambda b, i: (b, i, 0))
    fm = lambda r: pl.BlockSpec((None, r, TM_IN), lambda b, i: (b, 0, i))
    weights = [lw[k] for k in ("wtok", "wfeatT", "qng", "wuqT", "kvng", "wuk", "wuvT", "fb")]
    cst = [consts[k] for k in ("tri", "pk", "pq", "onesk", "onesq")]
    in_specs = [
        tok(D_MODEL),
        pl.BlockSpec((None, None, 1, D_MODEL), lambda b, i: (0, b, 0, 0)),
        pl.BlockSpec((None, None, 1, D_MODEL), lambda b, i: (1, b, 0, 0)),
        _const_spec((1, D_MODEL)),
        pl.BlockSpec((None, 1, TM_IN), lambda b, i: (b, 0, i)),
        _const_spec((HALF, 1)),
    ] + [_const_spec(w.shape, True) for w in weights] + [_const_spec(c.shape, True) for c in cst]
    fm_shape = lambda r: jax.ShapeDtypeStruct((BATCH, r, SEQ), BF16)
    tok_shape = lambda w: jax.ShapeDtypeStruct((BATCH, SEQ, w), BF16)
    out_shape = [fm_shape(A_HEADS * HP), tok_shape(A_HEADS * HP), fm_shape(A_WIDTH),
                 fm_shape(PW), tok_shape(PW), fm_shape(B_WIDTH),
                 fm_shape(PW), tok_shape(PW), fm_shape(C_WIDTH),
                 tok_shape(HP), fm_shape(HP), fm_shape(D_MIX)]
    out_specs = [fm(A_HEADS * HP), tok(A_HEADS * HP), fm(A_WIDTH),
                 fm(PW), tok(PW), fm(B_WIDTH),
                 fm(PW), tok(PW), fm(C_WIDTH),
                 tok(HP), fm(HP), fm(D_MIX)]
    return pl.pallas_call(
        _in_kernel,
        grid=(BATCH, SEQ // TM_IN),
        in_specs=in_specs,
        out_specs=out_specs,
        out_shape=out_shape,
        scratch_shapes=[pltpu.VMEM((1, HP), F32)],
        compiler_params=pltpu.CompilerParams(
            dimension_semantics=("parallel", "arbitrary"),
            vmem_limit_bytes=V7X_VMEM_LIMIT_BYTES),
        name="in_proj",
    )(x, mod_l, mod_l, lw["norm_g"], pos3, consts["inv"], *weights, *cst)


def _attn_kernel(*refs, band, paired, aug):
    if aug:
        qT_ref, k_ref, vT_ref, gate_ref, bias_ref, kaug_ref, qaug_ref, o_ref = refs[:8]
    else:
        qT_ref, k_ref, vT_ref, gate_ref, bias_ref, o_ref = refs[:6]
    s_bufs = refs[-4:-2]
    p_bufs = refs[-2:]
    hd = vT_ref.shape[0]
    head = pl.program_id(1)
    tile = lambda t: slice(t * TQ, (t + 1) * TQ)
    ktile = lambda c: slice(c * TK, (c + 1) * TK)
    row = lax.broadcasted_iota(jnp.int32, (HP, TQ), 0)

    def key_tiles(i):
        first = 0 if band is None else max(i - band, 0) * TQ // TK
        return range(first, (i + 1) * TQ // TK)

    def tile_bias(i, c):
        off = c * TK - i * TQ + (0 if band is None else band * TQ)
        return bias_ref[off:off + TK, :] if off >= 0 else None

    def query_operand(i):
        qT = qT_ref[:, tile(i)]
        if paired:
            qT = jnp.where(row // (HP // 2) == head % 2, qT.astype(F32), 0.0).astype(BF16)
        if aug:
            qa = jnp.where(row // SLOT == head, qaug_ref[:, tile(i)].astype(F32), 0.0)
            qT = jnp.concatenate([qT, qa.astype(BF16)], axis=0)
        return qT

    def score_steps(i, state):
        qT = query_operand(i)

        def step(c):
            kt = k_ref[ktile(c), :]
            if aug:
                kt = jnp.concatenate([kt, kaug_ref[ktile(c), :]], axis=1)
            s = _mm(kt, qT)
            bias = tile_bias(i, c)
            if bias is not None:
                s = s + bias
            s_bufs[slot[i]][c] = s
            pm = jnp.max(s.reshape(TK // 8, 8, TQ), axis=0)
            state["mx"] = pm if state["mx"] is None else jnp.maximum(state["mx"], pm)

        return [functools.partial(step, c) for c in key_tiles(i)]

    def exp_steps(i, state):
        m = jnp.max(state["mx"], axis=0, keepdims=True)
        state["l"] = jnp.zeros((8, TQ), F32)

        def step(c):
            p = jnp.exp2(s_bufs[slot[i]][c] - m)
            state["l"] = state["l"] + jnp.sum(p.reshape(TK // 8, 8, TQ), axis=0)
            p_bufs[slot[i]][c] = p.astype(BF16)

        return [functools.partial(step, c) for c in key_tiles(i)]

    def pv_steps(i, state):
        state["acc"] = jnp.zeros((hd, TQ), F32)

        def step(c):
            state["acc"] = state["acc"] + _mm(vT_ref[:, ktile(c)], p_bufs[slot[i]][c])

        return [functools.partial(step, c) for c in key_tiles(i)]

    def finish(i, state):
        l = jnp.sum(state["l"], axis=0, keepdims=True)
        o_ref[:, tile(i)] = (state["acc"] * (1.0 / l)
                             * gate_ref[:, tile(i)].astype(F32)).astype(BF16)

    order = []
    lo, hi = 1, NT - 1
    while lo <= hi:
        order += [lo, hi] if lo < hi else [lo]
        lo, hi = lo + 1, hi - 1
    order.append(0)
    slot = {q: n % 2 for n, q in enumerate(order)}
    states = {q: {"mx": None} for q in order}
    for t in range(NT + 2):
        streams = []
        if t < NT:
            streams.append(score_steps(order[t], states[order[t]]))
        if 0 <= t - 1 < NT:
            streams.append(exp_steps(order[t - 1], states[order[t - 1]]))
        if 0 <= t - 2 < NT:
            streams.append(pv_steps(order[t - 2], states[order[t - 2]]))
        for n in range(max(len(s) for s in streams)):
            for s in streams:
                if n < len(s):
                    s[n]()
        if 0 <= t - 2 < NT:
            finish(order[t - 2], states[order[t - 2]])


def _attn_call(qT, k, vT, gateT, bias, aug_ops=(), *, heads, hd, gate_row0, band, paired, name):
    if band is None:
        bias_spec = pl.BlockSpec(bias.shape, lambda b, h: (0, 0))
    else:
        bias_spec = pl.BlockSpec((None,) + bias.shape[1:], lambda b, h: (h, 0, 0))
    g0 = gate_row0 // hd
    blk = (lambda h: h // 2) if paired else (lambda h: h)
    aug_specs = []
    if aug_ops:
        aug_specs = [pl.BlockSpec((None, SEQ, HP), lambda b, h: (b, 0, 0)),
                     pl.BlockSpec((None, HP, SEQ), lambda b, h: (b, 0, 0))]
    return pl.pallas_call(
        functools.partial(_attn_kernel, band=band, paired=paired, aug=bool(aug_ops)),
        grid=(BATCH, heads),
        in_specs=[
            pl.BlockSpec((None, HP, SEQ), lambda b, h: (b, blk(h), 0)),
            pl.BlockSpec((None, SEQ, HP), lambda b, h: (b, 0, blk(h))),
            pl.BlockSpec((None, hd, SEQ), lambda b, h: (b, h, 0)),
            pl.BlockSpec((None, hd, SEQ), lambda b, h: (b, g0 + h, 0)),
            bias_spec,
        ] + aug_specs,
        out_specs=pl.BlockSpec((None, hd, SEQ), lambda b, h: (b, h, 0)),
        out_shape=jax.ShapeDtypeStruct((BATCH, heads * hd, SEQ), BF16),
        scratch_shapes=([pltpu.VMEM((SEQ // TK, TK, TQ), F32)] * 2
                        + [pltpu.VMEM((SEQ // TK, TK, TQ), BF16)] * 2),
        compiler_params=pltpu.CompilerParams(dimension_semantics=("parallel", "parallel")),
        name=name,
    )(qT, k, vT, gateT, bias, *aug_ops)


def _bias_kernel(v_ref, mask_ref, o_ref):
    kk = lax.broadcasted_iota(jnp.int32, (TQ, 2 * TQ), 0)

    def toeplitz(row):
        x = jnp.broadcast_to(row, (TQ, 2 * TQ))
        for bit in range(TQ.bit_length() - 1):
            x = jnp.where(((kk >> bit) & 1) == 1, pltpu.roll(x, 1 << bit, axis=1), x)
        return x[:, TQ:]

    v0 = v_ref[0]
    v1 = v_ref[1]
    far = jnp.broadcast_to(v1[:, 2 * TQ - 1:], (TQ, TQ))
    for t, tile in enumerate((far, toeplitz(v1), toeplitz(v0))):
        rows = slice(t * TQ, (t + 1) * TQ)
        o_ref[rows, :] = tile * LOG2E + mask_ref[rows, :]


def _band_bias(rel_bias_l, mask_b):
    assert B_BAND_TILES == 2 and TQ >= 2 * REL_CLIP
    lo = jnp.repeat(rel_bias_l[:, :1], REL_CLIP, axis=1)
    hi = rel_bias_l[:, 2 * REL_CLIP:]
    v0 = jnp.concatenate([lo, rel_bias_l, jnp.repeat(hi, 2 * TQ - 3 * REL_CLIP - 1, axis=1)], axis=1)
    v1 = jnp.concatenate([rel_bias_l[:, REL_CLIP:], jnp.repeat(hi, 2 * TQ - REL_CLIP - 1, axis=1)], axis=1)
    v = jnp.stack([v0, v1], axis=1).reshape(B_HEADS, 2, 1, 2 * TQ)
    nrow = (B_BAND_TILES + 1) * TQ
    return pl.pallas_call(
        _bias_kernel,
        grid=(B_HEADS,),
        in_specs=[pl.BlockSpec((None, 2, 1, 2 * TQ), lambda h: (h, 0, 0, 0)),
                  pl.BlockSpec((nrow, TQ), lambda h: (0, 0))],
        out_specs=pl.BlockSpec((None, nrow, TQ), lambda h: (h, 0, 0)),
        out_shape=jax.ShapeDtypeStruct((B_HEADS, nrow, TQ), F32),
        name="band_bias",
    )(v, mask_b)


def _out_kernel(mA_ref, mB_ref, mC_ref, woT_ref, x_ref, gate_ref, fg_ref, o_ref, *, final):
    for t in range(TM_OUT // TQ):
        cols = slice(t * TQ, (t + 1) * TQ)
        mixT = jnp.concatenate([mA_ref[:, cols], mB_ref[:, cols], mC_ref[:, cols]], axis=0)
        y = _mm(woT_ref[...], mixT).T
        xn = x_ref[cols, :] + gate_ref[...] * y
        if final:
            xn = _rms(xn, fg_ref[...])
        o_ref[cols, :] = xn


def _out_call(mA, mB, mC, woT, x, mod_l, final_g, *, final):
    fm = lambda r: pl.BlockSpec((None, r, TM_OUT), lambda b, i: (b, 0, i))
    tok = pl.BlockSpec((None, TM_OUT, D_MODEL), lambda b, i: (b, i, 0))
    return pl.pallas_call(
        functools.partial(_out_kernel, final=final),
        grid=(BATCH, SEQ // TM_OUT),
        in_specs=[fm(A_WIDTH), fm(B_WIDTH), fm(C_WIDTH),
                  _const_spec((D_MODEL, D_MIX)), tok,
                  pl.BlockSpec((None, None, 1, D_MODEL), lambda b, i: (2, b, 0, 0)),
                  _const_spec((1, D_MODEL))],
        out_specs=tok,
        out_shape=jax.ShapeDtypeStruct((BATCH, SEQ, D_MODEL), F32),
        compiler_params=pltpu.CompilerParams(dimension_semantics=("parallel", "parallel")),
        name="out_proj",
    )(mA, mB, mC, woT, x, mod_l, final_g)


def _transpose_kernel(w_ref, o_ref):
    o_ref[...] = w_ref[...].T.astype(BF16)


def _transpose_bf16(w):
    k, n = w.shape
    bn = 2 * HP if n % (2 * HP) == 0 else HP
    return pl.pallas_call(
        _transpose_kernel,
        grid=(n // bn,),
        in_specs=[pl.BlockSpec((k, bn), lambda j: (0, j))],
        out_specs=pl.BlockSpec((bn, k), lambda j: (j, 0)),
        out_shape=jax.ShapeDtypeStruct((n, k), BF16),
        name="w_transpose",
    )(w)


def _pad_cols(w, width):
    return jnp.pad(w, ((0, 0), (0, width - w.shape[1])))


def _constants():
    tri = np.tril(np.ones((TM_IN, TM_IN), np.float32))
    pk = np.zeros((HP, HP), np.float32)
    pq = np.zeros((HP, HP), np.float32)
    onesk = np.zeros((1, HP), np.float32)
    onesq = np.zeros((HP, 1), np.float32)
    for hh in range(C_HEADS):
        for j in range(3):
            onesk[0, hh * SLOT + j] = 1.0
            pq[hh * SLOT + j, j * SLOT + hh] = 1.0
            pk[j * SLOT + hh, hh * SLOT + 3 + j] = -1.0
            onesq[hh * SLOT + 3 + j, 0] = 1.0
    kk = np.arange(TQ)[:, None]
    qq = np.arange(TQ)[None, :]
    mask_a = np.where(kk // CHUNK <= qq // CHUNK, 0.0, NEG).astype(np.float32)
    mask_c = np.where(kk <= qq, 0.0, NEG).astype(np.float32)
    d = np.arange(B_BAND_TILES, -1, -1)[:, None, None]
    cdist = (TQ // CHUNK) * d + qq[None] // CHUNK - kk[None] // CHUNK
    mask_b = np.where((cdist >= 0) & (cdist <= B_LEFT_CHUNKS), 0.0, NEG).astype(np.float32)
    inv = ROPE_THETA ** (-jnp.arange(0, A_ROPE, 2, dtype=F32) / A_ROPE)
    return {
        "tri": jnp.asarray(tri, BF16), "pk": jnp.asarray(pk, BF16), "pq": jnp.asarray(pq, BF16),
        "onesk": jnp.asarray(onesk), "onesq": jnp.asarray(onesq),
        "mask_a": jnp.asarray(mask_a), "mask_c": jnp.asarray(mask_c),
        "mask_b": jnp.asarray(mask_b.reshape(-1, TQ)), "inv": inv.reshape(HALF, 1),
    }


def _layer_weights(l, norm_g, w_in, a_q_norm_g, a_w_uq, a_kv_norm_g, a_w_ukv,
                   b_rel_bias, c_forget_b, w_out, consts):
    w = w_in[l]
    col = lambda n: w[:, _OFFS[n]:_OFFS[n + 1]]
    w_gate = jnp.concatenate([col(3), col(7), col(12)], axis=1)
    uq = jnp.pad(a_w_uq[l].reshape(A_Q_RANK, A_HEADS, A_NOPE + A_ROPE),
                 ((0, 0), (0, 0), (0, HP - A_NOPE - A_ROPE))).reshape(A_Q_RANK, A_HEADS * HP)
    ukv = a_w_ukv[l].reshape(A_KV_RANK, A_HEADS, A_NOPE + A_V)
    uk = jnp.pad(ukv[:, :, :A_NOPE], ((0, 0), (0, 0), (0, HP - A_NOPE))).reshape(A_KV_RANK, A_HEADS * HP)
    uv = ukv[:, :, A_NOPE:].reshape(A_KV_RANK, A_WIDTH)
    kpe = jnp.pad(col(2), ((0, 0), (A_NOPE, HP - A_NOPE - A_ROPE)))
    feat = {"gate": w_gate, "qB": _pad_cols(col(4), PW), "vB": col(6),
            "qC": _pad_cols(col(8), PW), "vC": col(10), "kpe": kpe}
    tok = {"cq": col(0), "ckv": col(1), "kB": _pad_cols(col(5), PW),
           "kC": _pad_cols(col(9), PW), "cf": _pad_cols(col(11), HP)}
    return {
        "norm_g": norm_g[l].reshape(1, D_MODEL),
        "wtok": jnp.concatenate([tok[n] for n, _ in _TOK], axis=1).astype(BF16),
        "wfeatT": _transpose_bf16(jnp.concatenate([feat[n] for n, _ in _FEAT], axis=1)),
        "qng": a_q_norm_g[l].reshape(1, A_Q_RANK), "wuqT": _transpose_bf16(uq),
        "kvng": a_kv_norm_g[l].reshape(1, A_KV_RANK), "wuk": uk.astype(BF16),
        "wuvT": _transpose_bf16(uv),
        "fb": jnp.pad(c_forget_b[l], (0, HP - C_HEADS)).reshape(1, HP),
        "woT": _transpose_bf16(w_out[l]),
        "bias_b": _band_bias(b_rel_bias[l], consts["mask_b"]),
    }


def kernel(x, c, positions, w_ada, b_ada, norm_g, w_in, a_q_norm_g, a_w_uq, a_kv_norm_g,
           a_w_ukv, b_rel_bias, c_forget_b, w_out, final_g):
    consts = _constants()
    mod = _modulation(c, w_ada, b_ada).reshape(DEPTH, 3, BATCH, 1, D_MODEL)
    pos3 = positions.reshape(BATCH, 1, SEQ)
    fg = final_g.reshape(1, D_MODEL)
    for l in range(DEPTH):
        lw = _layer_weights(l, norm_g, w_in, a_q_norm_g, a_w_uq, a_kv_norm_g, a_w_ukv,
                            b_rel_bias, c_forget_b, w_out, consts)
        (qA, kA, vA, qB, kB, vB, qC, kC, vC, kaug, qaug, gateT) = _in_call(
            x, mod[l], pos3, consts, lw)
        mA = _attn_call(qA, kA, vA, gateT, consts["mask_a"], heads=A_HEADS, hd=A_V,
                        gate_row0=0, band=None, paired=False, name="attn_mla")
        mB = _attn_call(qB, kB, vB, gateT, lw["bias_b"], heads=B_HEADS, hd=B_HD,
                        gate_row0=A_WIDTH, band=B_BAND_TILES, paired=True, name="attn_band")
        mC = _attn_call(qC, kC, vC, gateT, consts["mask_c"], (kaug, qaug), heads=C_HEADS,
                        hd=C_HD, gate_row0=A_WIDTH + B_WIDTH, band=None, paired=True,
                        name="attn_forget")
        x = _out_call(mA, mB, mC, lw["woT"], x, mod[l], fg, final=(l == DEPTH - 1))
    return x
```

```python
import functools

import numpy as np
import jax
import jax.numpy as jnp
from jax import lax
from jax.experimental import pallas as pl
from jax.experimental.pallas import tpu as pltpu

F32 = jnp.float32
BF16 = jnp.bfloat16

D_MODEL = 1024
BATCH = 16
SEQ = 2048
DEPTH = 2
CHUNK = 64
EPS = 1e-6
NEG = -1e30

A_HEADS, A_NOPE, A_ROPE, A_V = 6, 64, 32, 64
A_Q_RANK, A_KV_RANK = 384, 256
A_WIDTH = A_HEADS * A_V
ROPE_THETA = 10000.0
B_HEADS, B_HD = 5, 64
B_WIDTH = B_HEADS * B_HD
B_LEFT_CHUNKS = 8
REL_CLIP = 128
C_HEADS, C_HD = 5, 64
C_WIDTH = C_HEADS * C_HD
D_MIX = A_WIDTH + B_WIDTH + C_WIDTH

_SIZES = (A_Q_RANK, A_KV_RANK, A_ROPE, A_WIDTH, B_WIDTH, B_WIDTH, B_WIDTH, B_WIDTH,
          C_WIDTH, C_WIDTH, C_WIDTH, C_HEADS, C_WIDTH)
_OFFS = tuple(int(v) for v in np.cumsum((0,) + _SIZES))

LOG2E = 1.4426950408889634
HP = 128
PAIRS = 3
PW = PAIRS * HP
TQ = 256
TK = 256
NT = SEQ // TQ
A_GROUP = 3
TM_IN = 512
TM_OUT = 512
HALF = A_ROPE // 2
SLOT = 8
B_BAND_TILES = (B_LEFT_CHUNKS * CHUNK) // TQ

V7X_VMEM_LIMIT_BYTES = 56 * 1024 * 1024

_FEAT = (("gate", D_MIX), ("qB", PW), ("vB", B_WIDTH), ("qC", PW), ("vC", C_WIDTH), ("kpe", HP))
_FEAT_OFF = dict(zip([n for n, _ in _FEAT], np.cumsum([0] + [r for _, r in _FEAT])[:-1].tolist()))
_FEAT_ROWS = dict(_FEAT)
_TOK = (("cq", A_Q_RANK), ("ckv", A_KV_RANK), ("kB", PW), ("kC", PW), ("cf", HP))
_TOK_OFF = dict(zip([n for n, _ in _TOK], np.cumsum([0] + [r for _, r in _TOK])[:-1].tolist()))
_TOK_COLS = dict(_TOK)


def _mm(a, b):
    return jnp.dot(a, b, preferred_element_type=F32)


def _nt(a, b):
    return lax.dot_general(a, b, (((1,), (1,)), ((), ())), preferred_element_type=F32)


def _rms(x, g):
    return x * lax.rsqrt(jnp.mean(x * x, axis=-1, keepdims=True) + EPS) * g


def _silu(x):
    return x / (1.0 + jnp.exp(-x))


def _split3(x):
    p1 = x.astype(BF16).astype(F32)
    r1 = x - p1
    p2 = r1.astype(BF16).astype(F32)
    p3 = (r1 - p2).astype(BF16).astype(F32)
    return p1, p2, p3


def _mod_kernel(c_ref, w_ref, b_ref, o_ref):
    ca = _silu(c_ref[...])
    w = w_ref[...]
    ch = ca.astype(BF16)
    cl = (ca - ch.astype(F32)).astype(BF16)
    wh = w.astype(BF16)
    wl = (w - wh.astype(F32)).astype(BF16)
    o_ref[...] = _mm(ch, wh) + _mm(ch, wl) + _mm(cl, wh) + b_ref[...]


def _modulation(c, w_ada, b_ada):
    return pl.pallas_call(
        _mod_kernel,
        grid=(DEPTH, 3),
        in_specs=[
            pl.BlockSpec((BATCH, D_MODEL), lambda l, j: (0, 0)),
            pl.BlockSpec((None, D_MODEL, D_MODEL), lambda l, j: (l, 0, j)),
            pl.BlockSpec((None, None, 1, D_MODEL), lambda l, j: (l, j, 0, 0)),
        ],
        out_specs=pl.BlockSpec((None, None, BATCH, D_MODEL), lambda l, j: (l, j, 0, 0)),
        out_shape=jax.ShapeDtypeStruct((DEPTH, 3, BATCH, D_MODEL), F32),
        name="adaln_mod",
    )(c, w_ada, b_ada.reshape(DEPTH, 3, 1, D_MODEL))


def _in_kernel(x_ref, shift_ref, scale_ref, g_ref, pos_ref, inv_ref,
               wtok_ref, wfeatT_ref, qng_ref, wuqT_ref, kvng_ref, wuk_ref, wuvT_ref,
               fb_ref, tri_ref, pk_ref, pq_ref, onesk_ref, onesq_ref,
               qA_ref, kA_ref, vA_ref, qB_ref, kB_ref, vB_ref, qC_ref, kC_ref, vC_ref,
               kaug_ref, qaug_ref, gA_ref, gB_ref, gC_ref, carry_ref):
    x = x_ref[...]
    h = _rms(x, g_ref[...]) * (1.0 + scale_ref[...]) + shift_ref[...]
    hb = h.astype(BF16)

    def feat(name, r0=0, r1=None):
        base = _FEAT_OFF[name]
        r1 = _FEAT_ROWS[name] if r1 is None else r1
        return _nt(wfeatT_ref[base + r0:base + r1, :], hb)

    ztok = _mm(hb, wtok_ref[...])
    tok = lambda name: ztok[:, _TOK_OFF[name]:_TOK_OFF[name] + _TOK_COLS[name]]

    bounds = np.cumsum([0, A_WIDTH, B_WIDTH, C_WIDTH])
    for r in range(0, D_MIX, 256):
        gt = _silu(feat("gate", r, r + 256)).astype(BF16)
        for g_ref, lo, hi in zip((gA_ref, gB_ref, gC_ref), bounds[:-1], bounds[1:]):
            a, b = max(r, lo), min(r + 256, hi)
            if a < b:
                g_ref[a - lo:b - lo, :] = gt[a - r:b - r]

    ang = inv_ref[...] * pos_ref[...].astype(F32)
    cos = jnp.cos(ang)
    sin = jnp.sin(ang)

    cqn = _rms(tok("cq"), qng_ref[...]).astype(BF16)
    a_scale = (A_NOPE + A_ROPE) ** -0.5 * LOG2E
    for hh in range(A_HEADS):
        base = hh * HP
        qh = _nt(wuqT_ref[base:base + HP, :], cqn) * a_scale
        x1 = qh[A_NOPE:A_NOPE + HALF]
        x2 = qh[A_NOPE + HALF:A_NOPE + A_ROPE]
        qA_ref[base:base + HP, :] = jnp.concatenate(
            [qh[:A_NOPE], x1 * cos - x2 * sin, x1 * sin + x2 * cos,
             jnp.zeros((HP - A_NOPE - A_ROPE, TM_IN), F32)], axis=0).astype(BF16)

    ckvn = _rms(tok("ckv"), kvng_ref[...]).astype(BF16)
    vA_ref[...] = _nt(wuvT_ref[...], ckvn).astype(BF16)
    kpeT = feat("kpe")
    x1 = kpeT[A_NOPE:A_NOPE + HALF]
    x2 = kpeT[A_NOPE + HALF:A_NOPE + A_ROPE]
    kpe_tok = jnp.concatenate(
        [jnp.zeros((A_NOPE, TM_IN), F32), x1 * cos - x2 * sin, x1 * sin + x2 * cos,
         jnp.zeros((HP - A_NOPE - A_ROPE, TM_IN), F32)], axis=0).T
    kn = _mm(ckvn, wuk_ref[...])
    for hh in range(A_HEADS):
        sl = slice(hh * HP, (hh + 1) * HP)
        kA_ref[:, sl] = (kn[:, sl] + kpe_tok).astype(BF16)

    bc_scale = B_HD ** -0.5 * LOG2E
    qB_ref[...] = (feat("qB") * bc_scale).astype(BF16)
    kB_ref[...] = tok("kB").astype(BF16)
    vB_ref[...] = feat("vB").astype(BF16)
    qC_ref[...] = (feat("qC") * bc_scale).astype(BF16)
    kC_ref[...] = tok("kC").astype(BF16)
    vC_ref[...] = feat("vC").astype(BF16)

    @pl.when(pl.program_id(1) == 0)
    def _():
        carry_ref[...] = jnp.zeros_like(carry_ref)

    lane = lax.broadcasted_iota(jnp.int32, (TM_IN, HP), 1)
    head_lane = lane < C_HEADS

    def pack3(v):
        p1, p2, p3 = (jnp.where(head_lane, p, 0.0) for p in _split3(v))
        return (p1 + pltpu.roll(p2, SLOT, axis=1) + pltpu.roll(p3, 2 * SLOT, axis=1)).astype(BF16)

    cf = tok("cf") + fb_ref[...]
    logf = -(jnp.maximum(-cf, 0.0) + jnp.log1p(jnp.exp(-jnp.abs(cf))))
    fl = _mm(tri_ref[...], pack3(logf))
    f_cum = (fl + pltpu.roll(fl, HP - SLOT, axis=1) + pltpu.roll(fl, HP - 2 * SLOT, axis=1)
             + carry_ref[...])
    carry_ref[...] = f_cum[TM_IN - 1:TM_IN, :]
    fs = pack3(f_cum * LOG2E)
    kaug_ref[...] = (_mm(fs, pk_ref[...]) + onesk_ref[...]).astype(BF16)
    qaug_ref[...] = (_nt(pq_ref[...], fs) + onesq_ref[...]).astype(BF16)


def _const_spec(shape, single_buffer=False):
    nd = len(shape)
    mode = pl.Buffered(1) if single_buffer else None
    return pl.BlockSpec(shape, lambda b, i, _nd=nd: (0,) * _nd, pipeline_mode=mode)


def _in_call(x, mod_l, pos3, consts, lw):
    tok = lambda w: pl.BlockSpec((None, TM_IN, w), lambda b, i: (b, i, 0))
    fm = lambda r: pl.BlockSpec((None, r, TM_IN), lambda b, i: (b, 0, i))
    weights = [lw[k] for k in ("wtok", "wfeatT", "qng", "wuqT", "kvng", "wuk", "wuvT", "fb")]
    cst = [consts[k] for k in ("tri", "pk", "pq", "onesk", "onesq")]
    in_specs = [
        tok(D_MODEL),
        pl.BlockSpec((None, None, 1, D_MODEL), lambda b, i: (0, b, 0, 0)),
        pl.BlockSpec((None, None, 1, D_MODEL), lambda b, i: (1, b, 0, 0)),
        _const_spec((1, D_MODEL)),
        pl.BlockSpec((None, 1, TM_IN), lambda b, i: (b, 0, i)),
        _const_spec((HALF, 1)),
    ] + [_const_spec(w.shape, True) for w in weights] + [_const_spec(c.shape, True) for c in cst]
    fm_shape = lambda r: jax.ShapeDtypeStruct((BATCH, r, SEQ), BF16)
    tok_shape = lambda w: jax.ShapeDtypeStruct((BATCH, SEQ, w), BF16)
    out_shape = [fm_shape(A_HEADS * HP), tok_shape(A_HEADS * HP), fm_shape(A_WIDTH),
                 fm_shape(PW), tok_shape(PW), fm_shape(B_WIDTH),
                 fm_shape(PW), tok_shape(PW), fm_shape(C_WIDTH),
                 tok_shape(HP), fm_shape(HP),
                 fm_shape(A_WIDTH), fm_shape(B_WIDTH), fm_shape(C_WIDTH)]
    out_specs = [fm(A_HEADS * HP), tok(A_HEADS * HP), fm(A_WIDTH),
                 fm(PW), tok(PW), fm(B_WIDTH),
                 fm(PW), tok(PW), fm(C_WIDTH),
                 tok(HP), fm(HP), fm(A_WIDTH), fm(B_WIDTH), fm(C_WIDTH)]
    return pl.pallas_call(
        _in_kernel,
        grid=(BATCH, SEQ // TM_IN),
        in_specs=in_specs,
        out_specs=out_specs,
        out_shape=out_shape,
        scratch_shapes=[pltpu.VMEM((1, HP), F32)],
        compiler_params=pltpu.CompilerParams(
            dimension_semantics=("parallel", "arbitrary"),
            vmem_limit_bytes=V7X_VMEM_LIMIT_BYTES),
        name="in_proj",
    )(x, mod_l, mod_l, lw["norm_g"], pos3, consts["inv"], *weights, *cst)


def _attn_kernel(*refs, band, paired, aug, hd, group):
    if aug:
        qT_ref, k_ref, vT_ref, gate_ref, bias_ref, kaug_ref, qaug_ref, o_ref = refs[:8]
    else:
        qT_ref, k_ref, vT_ref, gate_ref, bias_ref, o_ref = refs[:6]
    s_bufs = refs[-4:-2]
    p_bufs = refs[-2:]
    tile = lambda t: slice(t * TQ, (t + 1) * TQ)
    ktile = lambda c: slice(c * TK, (c + 1) * TK)
    hrows = lambda g: slice(g * hd, (g + 1) * hd)
    row = lax.broadcasted_iota(jnp.int32, (HP, TQ), 0)

    def key_tiles(i):
        first = 0 if band is None else max(i - band, 0) * TQ // TK
        return range(first, (i + 1) * TQ // TK)

    def tile_bias(g, i, c):
        off = c * TK - i * TQ + (0 if band is None else band * TQ)
        if off < 0:
            return None
        return bias_ref[off:off + TK, :] if band is None else bias_ref[g, off:off + TK, :]

    def operand_block(g):
        return slice((g // 2 if paired else g) * HP, (g // 2 if paired else g) * HP + HP)

    def query_operand(g, i):
        qT = qT_ref[operand_block(g), tile(i)]
        if paired:
            zeros = jnp.zeros((HP // 2, TQ), BF16)
            qT = jnp.concatenate([qT[:HP // 2], zeros] if g % 2 == 0 else [zeros, qT[HP // 2:]],
                                 axis=0)
        if aug:
            qa = jnp.where(row // SLOT == g, qaug_ref[:, tile(i)].astype(F32), 0.0)
            qT = jnp.concatenate([qT, qa.astype(BF16)], axis=0)
        return qT

    def score_steps(n, state):
        g, i = items[n]
        qT = query_operand(g, i)

        def step(c):
            kt = k_ref[ktile(c), operand_block(g)]
            if aug:
                kt = jnp.concatenate([kt, kaug_ref[ktile(c), :]], axis=1)
            s = _mm(kt, qT)
            bias = tile_bias(g, i, c)
            if bias is not None:
                s = s + bias
            s_bufs[n % 2][c] = s
            pm = jnp.max(s.reshape(TK // 8, 8, TQ), axis=0)
            state["mx"] = pm if state["mx"] is None else jnp.maximum(state["mx"], pm)

        return [functools.partial(step, c) for c in key_tiles(i)]

    def exp_steps(n, state):
        m = jnp.max(state["mx"], axis=0, keepdims=True)
        state["l"] = jnp.zeros((8, TQ), F32)

        def step(c):
            p = jnp.exp2(s_bufs[n % 2][c] - m)
            state["l"] = state["l"] + jnp.sum(p.reshape(TK // 8, 8, TQ), axis=0)
            p_bufs[n % 2][c] = p.astype(BF16)

        return [functools.partial(step, c) for c in key_tiles(items[n][1])]

    def pv_steps(n, state):
        g, i = items[n]
        state["acc"] = jnp.zeros((hd, TQ), F32)

        def step(c):
            state["acc"] = state["acc"] + _mm(vT_ref[hrows(g), ktile(c)], p_bufs[n % 2][c])

        return [functools.partial(step, c) for c in key_tiles(i)]

    def finish(n, state):
        g, i = items[n]
        l = jnp.sum(state["l"], axis=0, keepdims=True)
        o_ref[hrows(g), tile(i)] = (state["acc"] * (1.0 / l)
                                    * gate_ref[hrows(g), tile(i)].astype(F32)).astype(BF16)

    order = []
    lo, hi = 1, NT - 1
    while lo <= hi:
        order += [lo, hi] if lo < hi else [lo]
        lo, hi = lo + 1, hi - 1
    order.append(0)
    items = [(g, i) for g in range(group) for i in order]
    states = [{"mx": None} for _ in items]
    for t in range(len(items) + 2):
        streams = []
        if t < len(items):
            streams.append(score_steps(t, states[t]))
        if 0 <= t - 1 < len(items):
            streams.append(exp_steps(t - 1, states[t - 1]))
        if 0 <= t - 2 < len(items):
            streams.append(pv_steps(t - 2, states[t - 2]))
        for n in range(max(len(s) for s in streams)):
            for s in streams:
                if n < len(s):
                    s[n]()
        if 0 <= t - 2 < len(items):
            finish(t - 2, states[t - 2])


def _attn_call(qT, k, vT, gateT, bias, aug_ops=(), *, heads, hd, group, band, paired, name):
    assert heads % group == 0 and (group == heads or not (paired or aug_ops or band))
    width = qT.shape[1] // (heads // group)
    if band is None:
        bias_spec = pl.BlockSpec(bias.shape, lambda b, s: (0, 0))
    else:
        bias_spec = pl.BlockSpec(bias.shape, lambda b, s: (0, 0, 0), pipeline_mode=pl.Buffered(1))
    aug_specs = []
    if aug_ops:
        aug_specs = [pl.BlockSpec((None, SEQ, HP), lambda b, s: (b, 0, 0)),
                     pl.BlockSpec((None, HP, SEQ), lambda b, s: (b, 0, 0))]
    return pl.pallas_call(
        functools.partial(_attn_kernel, band=band, paired=paired, aug=bool(aug_ops),
                          hd=hd, group=group),
        grid=(BATCH, heads // group),
        in_specs=[
            pl.BlockSpec((None, width, SEQ), lambda b, s: (b, s, 0)),
            pl.BlockSpec((None, SEQ, width), lambda b, s: (b, 0, s)),
            pl.BlockSpec((None, group * hd, SEQ), lambda b, s: (b, s, 0)),
            pl.BlockSpec((None, group * hd, SEQ), lambda b, s: (b, s, 0)),
            bias_spec,
        ] + aug_specs,
        out_specs=pl.BlockSpec((None, group * hd, SEQ), lambda b, s: (b, s, 0)),
        out_shape=jax.ShapeDtypeStruct((BATCH, heads * hd, SEQ), BF16),
        scratch_shapes=([pltpu.VMEM((SEQ // TK, TK, TQ), F32)] * 2
                        + [pltpu.VMEM((SEQ // TK, TK, TQ), BF16)] * 2),
        compiler_params=pltpu.CompilerParams(
            dimension_semantics=("parallel", "parallel"),
            vmem_limit_bytes=V7X_VMEM_LIMIT_BYTES),
        name=name,
    )(qT, k, vT, gateT, bias, *aug_ops)


def _bias_kernel(v_ref, mask_ref, o_ref):
    kk = lax.broadcasted_iota(jnp.int32, (TQ, 2 * TQ), 0)

    def toeplitz(row):
        x = jnp.broadcast_to(row, (TQ, 2 * TQ))
        for bit in range(TQ.bit_length() - 1):
            x = jnp.where(((kk >> bit) & 1) == 1, pltpu.roll(x, 1 << bit, axis=1), x)
        return x[:, TQ:]

    v0 = v_ref[0]
    v1 = v_ref[1]
    far = jnp.broadcast_to(v1[:, 2 * TQ - 1:], (TQ, TQ))
    for t, tile in enumerate((far, toeplitz(v1), toeplitz(v0))):
        rows = slice(t * TQ, (t + 1) * TQ)
        o_ref[rows, :] = tile * LOG2E + mask_ref[rows, :]


def _band_bias(rel_bias_l, mask_b):
    assert B_BAND_TILES == 2 and TQ >= 2 * REL_CLIP
    lo = jnp.repeat(rel_bias_l[:, :1], REL_CLIP, axis=1)
    hi = rel_bias_l[:, 2 * REL_CLIP:]
    v0 = jnp.concatenate([lo, rel_bias_l, jnp.repeat(hi, 2 * TQ - 3 * REL_CLIP - 1, axis=1)], axis=1)
    v1 = jnp.concatenate([rel_bias_l[:, REL_CLIP:], jnp.repeat(hi, 2 * TQ - REL_CLIP - 1, axis=1)], axis=1)
    v = jnp.stack([v0, v1], axis=1).reshape(B_HEADS, 2, 1, 2 * TQ)
    nrow = (B_BAND_TILES + 1) * TQ
    return pl.pallas_call(
        _bias_kernel,
        grid=(B_HEADS,),
        in_specs=[pl.BlockSpec((None, 2, 1, 2 * TQ), lambda h: (h, 0, 0, 0)),
                  pl.BlockSpec((nrow, TQ), lambda h: (0, 0))],
        out_specs=pl.BlockSpec((None, nrow, TQ), lambda h: (h, 0, 0)),
        out_shape=jax.ShapeDtypeStruct((B_HEADS, nrow, TQ), F32),
        name="band_bias",
    )(v, mask_b)


def _out_kernel(mA_ref, mB_ref, mC_ref, woT_ref, x_ref, gate_ref, fg_ref, o_ref, *, final):
    for t in range(TM_OUT // TQ):
        cols = slice(t * TQ, (t + 1) * TQ)
        mixT = jnp.concatenate([mA_ref[:, cols], mB_ref[:, cols], mC_ref[:, cols]], axis=0)
        y = _mm(woT_ref[...], mixT).T
        xn = x_ref[cols, :] + gate_ref[...] * y
        if final:
            xn = _rms(xn, fg_ref[...])
        o_ref[cols, :] = xn


def _out_call(mA, mB, mC, woT, x, mod_l, final_g, *, final):
    fm = lambda r: pl.BlockSpec((None, r, TM_OUT), lambda b, i: (b, 0, i))
    tok = pl.BlockSpec((None, TM_OUT, D_MODEL), lambda b, i: (b, i, 0))
    return pl.pallas_call(
        functools.partial(_out_kernel, final=final),
        grid=(BATCH, SEQ // TM_OUT),
        in_specs=[fm(A_WIDTH), fm(B_WIDTH), fm(C_WIDTH),
                  _const_spec((D_MODEL, D_MIX)), tok,
                  pl.BlockSpec((None, None, 1, D_MODEL), lambda b, i: (2, b, 0, 0)),
                  _const_spec((1, D_MODEL))],
        out_specs=tok,
        out_shape=jax.ShapeDtypeStruct((BATCH, SEQ, D_MODEL), F32),
        compiler_params=pltpu.CompilerParams(dimension_semantics=("parallel", "parallel")),
        name="out_proj",
    )(mA, mB, mC, woT, x, mod_l, final_g)


def _transpose_kernel(w_ref, o_ref):
    o_ref[...] = w_ref[...].T.astype(BF16)


def _transpose_bf16(w):
    k, n = w.shape
    bn = 2 * HP if n % (2 * HP) == 0 else HP
    return pl.pallas_call(
        _transpose_kernel,
        grid=(n // bn,),
        in_specs=[pl.BlockSpec((k, bn), lambda j: (0, j))],
        out_specs=pl.BlockSpec((bn, k), lambda j: (j, 0)),
        out_shape=jax.ShapeDtypeStruct((n, k), BF16),
        name="w_transpose",
    )(w)


def _pad_cols(w, width):
    return jnp.pad(w, ((0, 0), (0, width - w.shape[1])))


def _constants():
    tri = np.tril(np.ones((TM_IN, TM_IN), np.float32))
    pk = np.zeros((HP, HP), np.float32)
    pq = np.zeros((HP, HP), np.float32)
    onesk = np.zeros((1, HP), np.float32)
    onesq = np.zeros((HP, 1), np.float32)
    for hh in range(C_HEADS):
        for j in range(3):
            onesk[0, hh * SLOT + j] = 1.0
            pq[hh * SLOT + j, j * SLOT + hh] = 1.0
            pk[j * SLOT + hh, hh * SLOT + 3 + j] = -1.0
            onesq[hh * SLOT + 3 + j, 0] = 1.0
    kk = np.arange(TQ)[:, None]
    qq = np.arange(TQ)[None, :]
    mask_a = np.where(kk // CHUNK <= qq // CHUNK, 0.0, NEG).astype(np.float32)
    mask_c = np.where(kk <= qq, 0.0, NEG).astype(np.float32)
    d = np.arange(B_BAND_TILES, -1, -1)[:, None, None]
    cdist = (TQ // CHUNK) * d + qq[None] // CHUNK - kk[None] // CHUNK
    mask_b = np.where((cdist >= 0) & (cdist <= B_LEFT_CHUNKS), 0.0, NEG).astype(np.float32)
    inv = ROPE_THETA ** (-jnp.arange(0, A_ROPE, 2, dtype=F32) / A_ROPE)
    return {
        "tri": jnp.asarray(tri, BF16), "pk": jnp.asarray(pk, BF16), "pq": jnp.asarray(pq, BF16),
        "onesk": jnp.asarray(onesk), "onesq": jnp.asarray(onesq),
        "mask_a": jnp.asarray(mask_a), "mask_c": jnp.asarray(mask_c),
        "mask_b": jnp.asarray(mask_b.reshape(-1, TQ)), "inv": inv.reshape(HALF, 1),
    }


def _layer_weights(l, norm_g, w_in, a_q_norm_g, a_w_uq, a_kv_norm_g, a_w_ukv,
                   b_rel_bias, c_forget_b, w_out, consts):
    w = w_in[l]
    col = lambda n: w[:, _OFFS[n]:_OFFS[n + 1]]
    w_gate = jnp.concatenate([col(3), col(7), col(12)], axis=1)
    uq = jnp.pad(a_w_uq[l].reshape(A_Q_RANK, A_HEADS, A_NOPE + A_ROPE),
                 ((0, 0), (0, 0), (0, HP - A_NOPE - A_ROPE))).reshape(A_Q_RANK, A_HEADS * HP)
    ukv = a_w_ukv[l].reshape(A_KV_RANK, A_HEADS, A_NOPE + A_V)
    uk = jnp.pad(ukv[:, :, :A_NOPE], ((0, 0), (0, 0), (0, HP - A_NOPE))).reshape(A_KV_RANK, A_HEADS * HP)
    uv = ukv[:, :, A_NOPE:].reshape(A_KV_RANK, A_WIDTH)
    kpe = jnp.pad(col(2), ((0, 0), (A_NOPE, HP - A_NOPE - A_ROPE)))
    feat = {"gate": w_gate, "qB": _pad_cols(col(4), PW), "vB": col(6),
            "qC": _pad_cols(col(8), PW), "vC": col(10), "kpe": kpe}
    tok = {"cq": col(0), "ckv": col(1), "kB": _pad_cols(col(5), PW),
           "kC": _pad_cols(col(9), PW), "cf": _pad_cols(col(11), HP)}
    return {
        "norm_g": norm_g[l].reshape(1, D_MODEL),
        "wtok": jnp.concatenate([tok[n] for n, _ in _TOK], axis=1).astype(BF16),
        "wfeatT": _transpose_bf16(jnp.concatenate([feat[n] for n, _ in _FEAT], axis=1)),
        "qng": a_q_norm_g[l].reshape(1, A_Q_RANK), "wuqT": _transpose_bf16(uq),
        "kvng": a_kv_norm_g[l].reshape(1, A_KV_RANK), "wuk": uk.astype(BF16),
        "wuvT": _transpose_bf16(uv),
        "fb": jnp.pad(c_forget_b[l], (0, HP - C_HEADS)).reshape(1, HP),
        "woT": _transpose_bf16(w_out[l]),
        "bias_b": _band_bias(b_rel_bias[l], consts["mask_b"]),
    }


def kernel(x, c, positions, w_ada, b_ada, norm_g, w_in, a_q_norm_g, a_w_uq, a_kv_norm_g,
           a_w_ukv, b_rel_bias, c_forget_b, w_out, final_g):
    consts = _constants()
    mod = _modulation(c, w_ada, b_ada).reshape(DEPTH, 3, BATCH, 1, D_MODEL)
    pos3 = positions.reshape(BATCH, 1, SEQ)
    fg = final_g.reshape(1, D_MODEL)
    for l in range(DEPTH):
        lw = _layer_weights(l, norm_g, w_in, a_q_norm_g, a_w_uq, a_kv_norm_g, a_w_ukv,
                            b_rel_bias, c_forget_b, w_out, consts)
        (qA, kA, vA, qB, kB, vB, qC, kC, vC, kaug, qaug, gA, gB, gC) = _in_call(
            x, mod[l], pos3, consts, lw)
        mA = _attn_call(qA, kA, vA, gA, consts["mask_a"], heads=A_HEADS, hd=A_V,
                        group=A_GROUP, band=None, paired=False, name="attn_mla")
        mB = _attn_call(qB, kB, vB, gB, lw["bias_b"], heads=B_HEADS, hd=B_HD,
                        group=B_HEADS, band=B_BAND_TILES, paired=True, name="attn_band")
        mC = _attn_call(qC, kC, vC, gC, consts["mask_c"], (kaug, qaug), heads=C_HEADS,
                        hd=C_HD, group=C_HEADS, band=None, paired=True, name="attn_forget")
        x = _out_call(mA, mB, mC, lw["woT"], x, mod[l], fg, final=(l == DEPTH - 1))
    return x
```
